```python
import jax
import jax.numpy as jnp
from jax import lax
import numpy as np

D_MODEL = 2048
BATCH = 4
SEQ = 4096
DEPTH = 4
DEC_BATCH = 8
DEC_SEQ = 16
PAST_LEN = 2048

CHUNK = 64
N_MIXERS = 3
N_RWKV_LAYERS = (DEPTH + 2) // 3
N_POOL_LAYERS = (DEPTH + 1) // 3
N_HGRN_LAYERS = DEPTH // 3

RWKV_HEAD = 64
RWKV_HEADS = D_MODEL // RWKV_HEAD
RWKV_DECAY_LORA = 96
RWKV_AAA_LORA = 96
RWKV_MV_LORA = 64
RWKV_GATE_LORA = 256
RWKV_GN_EPS = 64e-5

POOL_WINDOWS = (2, 4, 8, 16)
POOL_GROUP = D_MODEL // len(POOL_WINDOWS)
POOL_HIST = max(POOL_WINDOWS) - 1

HGRN_EXPAND = 128
HGRN_HEADS = D_MODEL // HGRN_EXPAND
HGRN_VDIM = D_MODEL // HGRN_HEADS

N_EXPERTS = 64
TOP_K = 8
D_EXPERT = 512
D_SHARED = 512
ROUTED_SCALE = 2.5
MOE_BLOCK = 128

ALPHA = (2 * DEPTH) ** 0.25
BETA = (8 * DEPTH) ** -0.25
LN_EPS = 1e-5

kernel_name = 'hybrid_rwkv7_pool_hgrn2_moe_stream_step'


def layer_norm(x, w, b):
    xf = x.astype(jnp.float32)
    mu = jnp.mean(xf, axis=-1, keepdims=True)
    var = jnp.mean(jnp.square(xf - mu), axis=-1, keepdims=True)
    return ((xf - mu) * lax.rsqrt(var + LN_EPS) * w + b).astype(x.dtype)


def swiglu(h, w_gate, w_up, w_down):
    return (jax.nn.silu(h @ w_gate) * (h @ w_up)) @ w_down


def routed_experts(h, top_idx, top_w, w_gate, w_up, w_down):
    n_tok, d = h.shape
    n_exp = w_gate.shape[0]
    k = top_idx.shape[1]
    n_assign = n_tok * k
    blk = MOE_BLOCK
    n_blocks = -(-(n_assign + n_exp * (blk - 1)) // blk)
    flat_e = top_idx.reshape(-1).astype(jnp.int32)
    order = jnp.argsort(flat_e).astype(jnp.int32)
    e_sorted = flat_e[order]
    counts = jnp.bincount(flat_e, length=n_exp).astype(jnp.int32)
    padded = (counts + blk - 1) // blk * blk
    pad_end = jnp.cumsum(padded)
    pad_start = pad_end - padded
    start = jnp.cumsum(counts) - counts
    dest = pad_start[e_sorted] + jnp.arange(n_assign, dtype=jnp.int32) - start[e_sorted]
    slot_tok = jnp.full((n_blocks * blk,), n_tok, jnp.int32).at[dest].set(order // k)
    slot_w = jnp.zeros((n_blocks * blk,), top_w.dtype).at[dest].set(top_w.reshape(-1)[order])
    blk_exp = jnp.minimum(jnp.searchsorted(pad_end, jnp.arange(n_blocks, dtype=jnp.int32) * blk, side='right'), n_exp - 1)
    h_pad = jnp.concatenate([h, jnp.zeros((1, d), h.dtype)], axis=0)

    def one_block(args):
        tok, wts, e = args
        y = swiglu(h_pad[tok], w_gate[e], w_up[e], w_down[e])
        return y * wts[:, None]

    y = lax.map(one_block, (slot_tok.reshape(n_blocks, blk), slot_w.reshape(n_blocks, blk), blk_exp))
    return jax.ops.segment_sum(y.reshape(-1, d), slot_tok, num_segments=n_tok + 1)[:n_tok]


def moe_ffn(h, i, p):
    B, T, D = h.shape
    h2 = h.reshape(B * T, D)
    scores = jax.nn.sigmoid((h2 @ p['router_w'][i]).astype(jnp.float32))
    _, top_idx = lax.top_k(scores + p['router_bias'][i].astype(jnp.float32), TOP_K)
    top_s = jnp.take_along_axis(scores, top_idx, axis=-1)
    top_w = (top_s / jnp.sum(top_s, axis=-1, keepdims=True) * ROUTED_SCALE).astype(h.dtype)
    routed = routed_experts(h2, top_idx, top_w, p['exp_w_gate'][i], p['exp_w_up'][i], p['exp_w_down'][i])
    shared = swiglu(h2, p['shared_w_gate'][i], p['shared_w_up'][i], p['shared_w_down'][i])
    return (routed + shared).reshape(B, T, D)


def rwkv7_mix(u, shift_prev, wkv0, v_first, j, p):
    B, T, D = u.shape
    H, N = RWKV_HEADS, RWKV_HEAD
    f32 = jnp.float32
    prev = jnp.concatenate([shift_prev[:, None, :], u[:, :-1]], axis=1)
    xx = prev - u
    mu = p['rwkv_mu'][j]
    xr, xw, xk, xv, xa, xg = [u + xx * mu[n] for n in range(6)]
    w_rkv = p['rwkv_w_rkv'][j]
    r = xr @ w_rkv[0]
    k = xk @ w_rkv[1]
    v = xv @ w_rkv[2]
    w_log = -jax.nn.softplus(-(p['rwkv_w0'][j] + jnp.tanh(xw @ p['rwkv_w1'][j]) @ p['rwkv_w2'][j])) - 0.5
    a = jax.nn.sigmoid(p['rwkv_a0'][j] + (xa @ p['rwkv_a1'][j]) @ p['rwkv_a2'][j])
    g = jax.nn.sigmoid(xg @ p['rwkv_g1'][j]) @ p['rwkv_g2'][j]
    if v_first is None:
        v_first = v
    else:
        v = v + (v_first - v) * jax.nn.sigmoid(p['rwkv_v0'][j - 1] + (xv @ p['rwkv_v1'][j - 1]) @ p['rwkv_v2'][j - 1])
    kk = k * p['rwkv_k_k'][j]
    k = k * (1.0 + (a - 1.0) * p['rwkv_k_a'][j])
    hd = lambda t: t.reshape(B, T, H, N).astype(f32)
    rf, kf, vf, af = hd(r), hd(k), hd(v), hd(a)
    kkf = hd(kk)
    kkf = kkf / jnp.maximum(jnp.sqrt(jnp.sum(kkf * kkf, axis=-1, keepdims=True)), 1e-12)
    decay = jnp.exp(-jnp.exp(hd(w_log)))

    def step(S, inp):
        r_t, w_t, k_t, v_t, kk_t, a_t = inp
        sa = jnp.einsum('bhvk,bhk->bhv', S, kk_t)
        S = S * w_t[:, :, None, :] - sa[..., None] * (kk_t * a_t)[:, :, None, :] + v_t[..., None] * k_t[:, :, None, :]
        return S, jnp.einsum('bhvk,bhk->bhv', S, r_t)

    tm = lambda t: jnp.moveaxis(t, 1, 0)
    S, o = lax.scan(step, wkv0.astype(f32), (tm(rf), tm(decay), tm(kf), tm(vf), tm(kkf), tm(af)))
    o = jnp.moveaxis(o, 0, 1)
    mean = jnp.mean(o, axis=-1, keepdims=True)
    var = jnp.mean(jnp.square(o - mean), axis=-1, keepdims=True)
    o = ((o - mean) * lax.rsqrt(var + RWKV_GN_EPS)).reshape(B, T, D) * p['rwkv_gn_w'][j] + p['rwkv_gn_b'][j]
    bonus = jnp.sum(rf * kf * p['rwkv_r_k'][j], axis=-1, keepdims=True) * vf
    o = (o + bonus.reshape(B, T, D)).astype(u.dtype) * g
    y = o @ p['rwkv_w_o'][j]
    return y, u[:, -1], S.astype(wkv0.dtype), v_first


def pool_mix(u, hist, pos0, j, p):
    B, T, D = u.shape
    f32 = jnp.float32
    full = jnp.concatenate([hist, u], axis=1)
    cs = jnp.concatenate([jnp.zeros((B, 1, D), f32), jnp.cumsum(full.astype(f32), axis=1)], axis=1)
    end = POOL_HIST + 1
    pos = pos0 + jnp.arange(T)
    outs = []
    for gi, win in enumerate(POOL_WINDOWS):
        sl = slice(gi * POOL_GROUP, (gi + 1) * POOL_GROUP)
        tot = cs[:, end:end + T, sl] - cs[:, end - win:end - win + T, sl]
        cnt = jnp.minimum(pos + 1, win).astype(f32)[None, :, None]
        pooled = tot / cnt - u[:, :, sl].astype(f32)
        outs.append(pooled.astype(u.dtype) @ p['pool_w'][j, gi])
    y = jnp.concatenate(outs, axis=-1) * p['pool_scale'][j]
    return y, full[:, -POOL_HIST:]


def hgrn2_recurrence(q, k, v, g, s0, blk):
    B, T, H, DK = q.shape
    DV = v.shape[-1]
    nb = T // blk

    def blocks(t):
        return t.reshape(B, nb, blk, H, t.shape[-1]).transpose(1, 0, 3, 2, 4)

    causal = jnp.tril(jnp.ones((blk, blk), dtype=bool))[:, :, None]

    def step(S, inp):
        qb, kb, vb, gb = inp
        b = jnp.cumsum(gb, axis=2)
        rel = jnp.exp(jnp.where(causal, b[:, :, :, None, :] - b[:, :, None, :, :], -jnp.inf))
        att = jnp.einsum('bhtd,bhsd,bhtsd->bhts', qb, kb, rel)
        o = jnp.einsum('bhts,bhsv->bhtv', att, vb) + jnp.einsum('bhtd,bhdv->bhtv', qb * jnp.exp(b), S)
        b_last = b[:, :, -1:, :]
        S = jnp.exp(b_last[:, :, 0, :])[..., None] * S + jnp.einsum('bhsd,bhsv->bhdv', kb * jnp.exp(b_last - b), vb)
        return S, o

    S, o = lax.scan(step, s0, (blocks(q), blocks(k), blocks(v), blocks(g)))
    o = o.transpose(1, 0, 3, 2, 4).reshape(B, T, H, DV)
    return o, S


def hgrn2_mix(u, s0, lb, j, p):
    B, T, D = u.shape
    H, DK, DV = HGRN_HEADS, HGRN_EXPAND, HGRN_VDIM
    f32 = jnp.float32
    z = u @ p['hgrn_w_in'][j]
    q, fz, iv, gz = jnp.split(z, 4, axis=-1)
    f = lb + (1.0 - lb) * jax.nn.sigmoid(fz.astype(f32))
    hd = lambda t, d: t.reshape(B, T, H, d).astype(f32)
    o, S = hgrn2_recurrence(hd(jax.nn.silu(q), DK), hd(1.0 - f, DK), hd(iv, DV), hd(jnp.log(f), DK),
                            s0.astype(f32), min(CHUNK, T))
    o = o * lax.rsqrt(jnp.mean(o * o, axis=-1, keepdims=True) + LN_EPS)
    o = o.reshape(B, T, D) * p['hgrn_norm_w'][j] * jax.nn.silu(gz.astype(f32))
    y = o.astype(u.dtype) @ p['hgrn_w_out'][j]
    return y, S.astype(s0.dtype)


def trunk(x, c, shift_in, wkv_in, pool_in, hgrn_in, pos0, p):
    lb_cum = jnp.cumsum(jax.nn.softmax(p['hgrn_lb'].astype(jnp.float32), axis=0), axis=0)
    lower_bounds = lb_cum - lb_cum[0]
    cmod = jax.nn.silu(c)
    shift_out, wkv_out, pool_out, hgrn_out = [], [], [], []
    v_first = None
    for i in range(DEPTH):
        j = i // N_MIXERS
        kind = i % N_MIXERS
        mod = cmod @ p['ada_w'][i] + p['ada_b'][i]
        sh1, sc1, g1, sh2, sc2, g2 = [m[:, None, :] for m in jnp.split(mod, 6, axis=-1)]
        u = x * (1.0 + sc1) + sh1
        if kind == 0:
            y, s_new, w_new, v_first = rwkv7_mix(u, shift_in[j], wkv_in[j], v_first, j, p)
            shift_out.append(s_new)
            wkv_out.append(w_new)
        elif kind == 1:
            y, pl_new = pool_mix(u, pool_in[j], pos0, j, p)
            pool_out.append(pl_new)
        else:
            y, h_new = hgrn2_mix(u, hgrn_in[j], lower_bounds[i], j, p)
            hgrn_out.append(h_new)
        x = layer_norm(ALPHA * x + (1.0 + g1) * y, p['ln_w'][i, 0], p['ln_b'][i, 0])
        h = x * (1.0 + sc2) + sh2
        x = layer_norm(ALPHA * x + (1.0 + g2) * moe_ffn(h, i, p), p['ln_w'][i, 1], p['ln_b'][i, 1])
    return x, jnp.stack(shift_out), jnp.stack(wkv_out), jnp.stack(pool_out), jnp.stack(hgrn_out)


def setup_inputs(seed: int = 0) -> dict:
    key = jax.random.key(seed)
    keys = jax.random.split(key, 64)
    counter = [0]

    def next_key():
        kk = keys[counter[0]]
        counter[0] += 1
        return kk

    def nrm(shape, scale):
        return jax.random.normal(next_key(), shape, jnp.float32) * scale

    def unif(shape, lo, hi):
        return jax.random.uniform(next_key(), shape, jnp.float32, lo, hi)

    D = D_MODEL
    NA, NB, NC = N_RWKV_LAYERS, N_POOL_LAYERS, N_HGRN_LAYERS
    H, N = RWKV_HEADS, RWKV_HEAD
    E, F, FS, G = N_EXPERTS, D_EXPERT, D_SHARED, POOL_GROUP
    rkv_scale = jnp.array([1.0, 1.0, BETA], jnp.float32)[None, :, None, None]
    hgrn_scale = jnp.concatenate([jnp.ones((2 * D,), jnp.float32), jnp.full((D,), BETA, jnp.float32),
                                  jnp.ones((D,), jnp.float32)])
    return {
        'x_prompt': nrm((BATCH, SEQ, D), 1.0),
        'x_sample': nrm((DEC_BATCH, DEC_SEQ, D), 1.0),
        'state_rwkv_shift': nrm((NA, DEC_BATCH, D), 1.0),
        'state_rwkv_wkv': nrm((NA, DEC_BATCH, H, N, N), 0.3),
        'state_pool': nrm((NB, DEC_BATCH, POOL_HIST, D), 1.0),
        'state_hgrn': nrm((NC, DEC_BATCH, HGRN_HEADS, HGRN_EXPAND, HGRN_VDIM), 0.3),
        'c_prompt': nrm((BATCH, D), 1.0),
        'c_sample': nrm((DEC_BATCH, D), 1.0),
        'ada_w': nrm((DEPTH, D, 6 * D), 0.2 * D ** -0.5),
        'ada_b': nrm((DEPTH, 6 * D), 0.01),
        'ln_w': 1.0 + nrm((DEPTH, 2, D), 0.05),
        'ln_b': nrm((DEPTH, 2, D), 0.01),
        'rwkv_mu': unif((NA, 6, D), 0.0, 1.0),
        'rwkv_w_rkv': nrm((NA, 3, D, D), D ** -0.5) * rkv_scale,
        'rwkv_w_o': nrm((NA, D, D), D ** -0.5 * BETA),
        'rwkv_w0': unif((NA, D), -3.0, 0.5),
        'rwkv_w1': nrm((NA, D, RWKV_DECAY_LORA), D ** -0.5),
        'rwkv_w2': nrm((NA, RWKV_DECAY_LORA, D), 0.1 * RWKV_DECAY_LORA ** -0.5),
        'rwkv_a0': nrm((NA, D), 0.1),
        'rwkv_a1': nrm((NA, D, RWKV_AAA_LORA), D ** -0.5),
        'rwkv_a2': nrm((NA, RWKV_AAA_LORA, D), 0.1 * RWKV_AAA_LORA ** -0.5),
        'rwkv_g1': nrm((NA, D, RWKV_GATE_LORA), D ** -0.5),
        'rwkv_g2': nrm((NA, RWKV_GATE_LORA, D), RWKV_GATE_LORA ** -0.5),
        'rwkv_v0': nrm((NA - 1, D), 0.1),
        'rwkv_v1': nrm((NA - 1, D, RWKV_MV_LORA), D ** -0.5),
        'rwkv_v2': nrm((NA - 1, RWKV_MV_LORA, D), 0.1 * RWKV_MV_LORA ** -0.5),
        'rwkv_k_k': 0.85 + nrm((NA, D), 0.02),
        'rwkv_k_a': 1.0 + nrm((NA, D), 0.02),
        'rwkv_r_k': nrm((NA, H, N), 0.1),
        'rwkv_gn_w': 1.0 + nrm((NA, D), 0.05),
        'rwkv_gn_b': nrm((NA, D), 0.01),
        'pool_w': nrm((NB, len(POOL_WINDOWS), G, G), G ** -0.5 * BETA),
        'pool_scale': 1.0 + nrm((NB, D), 0.05),
        'hgrn_w_in': nrm((NC, D, 4 * D), D ** -0.5) * hgrn_scale,
        'hgrn_norm_w': 1.0 + nrm((NC, D), 0.05),
        'hgrn_w_out': nrm((NC, D, D), D ** -0.5 * BETA),
        'hgrn_lb': nrm((DEPTH, D), 0.5),
        'router_w': nrm((DEPTH, D, E), D ** -0.5),
        'router_bias': nrm((DEPTH, E), 0.01),
        'exp_w_gate': nrm((DEPTH, E, D, F), D ** -0.5),
        'exp_w_up': nrm((DEPTH, E, D, F), D ** -0.5),
        'exp_w_down': nrm((DEPTH, E, F, D), F ** -0.5 * BETA),
        'shared_w_gate': nrm((DEPTH, D, FS), D ** -0.5),
        'shared_w_up': nrm((DEPTH, D, FS), D ** -0.5),
        'shared_w_down': nrm((DEPTH, FS, D), FS ** -0.5 * BETA),
    }


def reference(x_prompt, x_sample, state_rwkv_shift, state_rwkv_wkv, state_pool, state_hgrn,
              c_prompt, c_sample, ada_w, ada_b, ln_w, ln_b,
              rwkv_mu, rwkv_w_rkv, rwkv_w_o, rwkv_w0, rwkv_w1, rwkv_w2, rwkv_a0, rwkv_a1, rwkv_a2,
              rwkv_g1, rwkv_g2, rwkv_v0, rwkv_v1, rwkv_v2, rwkv_k_k, rwkv_k_a, rwkv_r_k,
              rwkv_gn_w, rwkv_gn_b, pool_w, pool_scale, hgrn_w_in, hgrn_norm_w, hgrn_w_out, hgrn_lb,
              router_w, router_bias, exp_w_gate, exp_w_up, exp_w_down,
              shared_w_gate, shared_w_up, shared_w_down):
    p = dict(ada_w=ada_w, ada_b=ada_b, ln_w=ln_w, ln_b=ln_b,
             rwkv_mu=rwkv_mu, rwkv_w_rkv=rwkv_w_rkv, rwkv_w_o=rwkv_w_o, rwkv_w0=rwkv_w0,
             rwkv_w1=rwkv_w1, rwkv_w2=rwkv_w2, rwkv_a0=rwkv_a0, rwkv_a1=rwkv_a1, rwkv_a2=rwkv_a2,
             rwkv_g1=rwkv_g1, rwkv_g2=rwkv_g2, rwkv_v0=rwkv_v0, rwkv_v1=rwkv_v1, rwkv_v2=rwkv_v2,
             rwkv_k_k=rwkv_k_k, rwkv_k_a=rwkv_k_a, rwkv_r_k=rwkv_r_k, rwkv_gn_w=rwkv_gn_w,
             rwkv_gn_b=rwkv_gn_b, pool_w=pool_w, pool_scale=pool_scale, hgrn_w_in=hgrn_w_in,
             hgrn_norm_w=hgrn_norm_w, hgrn_w_out=hgrn_w_out, hgrn_lb=hgrn_lb,
             router_w=router_w, router_bias=router_bias, exp_w_gate=exp_w_gate, exp_w_up=exp_w_up,
             exp_w_down=exp_w_down, shared_w_gate=shared_w_gate, shared_w_up=shared_w_up,
             shared_w_down=shared_w_down)
    b = x_prompt.shape[0]
    zero_shift = jnp.zeros((N_RWKV_LAYERS, b, D_MODEL), state_rwkv_shift.dtype)
    zero_wkv = jnp.zeros((N_RWKV_LAYERS, b, RWKV_HEADS, RWKV_HEAD, RWKV_HEAD), state_rwkv_wkv.dtype)
    zero_pool = jnp.zeros((N_POOL_LAYERS, b, POOL_HIST, D_MODEL), state_pool.dtype)
    zero_hgrn = jnp.zeros((N_HGRN_LAYERS, b, HGRN_HEADS, HGRN_EXPAND, HGRN_VDIM), state_hgrn.dtype)
    y_prompt, p_shift, p_wkv, p_pool, p_hgrn = trunk(x_prompt, c_prompt, zero_shift, zero_wkv,
                                                     zero_pool, zero_hgrn, 0, p)
    y_sample, s_shift, s_wkv, s_pool, s_hgrn = trunk(x_sample, c_sample, state_rwkv_shift, state_rwkv_wkv,
                                                     state_pool, state_hgrn, PAST_LEN, p)
    return (y_prompt, y_sample, p_shift, p_wkv, p_pool, p_hgrn, s_shift, s_wkv, s_pool, s_hgrn)
```

```python
import functools
import math

import numpy as np
import jax
import jax.numpy as jnp
from jax import lax
from jax.experimental import pallas as pl
from jax.experimental.pallas import tpu as pltpu

F32 = jnp.float32
BF16 = jnp.bfloat16

DEPTH = 4
N_MIXERS = 3
RWKV_HEAD = 64
RWKV_GN_EPS = 64e-5
POOL_WINDOWS = (2, 4, 8, 16)
POOL_HIST = max(POOL_WINDOWS) - 1
HGRN_EXPAND = 128
TOP_K = 8
ROUTED_SCALE = 2.5
ALPHA = (2 * DEPTH) ** 0.25
LN_EPS = 1e-5
CHUNK = 64
PAST_LEN = 2048

V7X_VMEM_BYTES = 64 * 1024 * 1024
LANES = 128
HEAD_GROUP = 4
GROUP_W = HEAD_GROUP * RWKV_HEAD
MOE_BM = 512
STEP_HEADS = 8


def _vmem(mb):
    return int(mb * 1024 * 1024)


def _bf(x):
    return x.astype(BF16)


def _dot(a, b):
    return jnp.dot(a, b, preferred_element_type=F32)


def _dot_nt(a, b):
    return lax.dot_general(a, b, (((1,), (1,)), ((), ())), preferred_element_type=F32)


def _dot_tn(a, b):
    return lax.dot_general(a, b, (((0,), (0,)), ((), ())), preferred_element_type=F32)


def _dot01(m_bf, x):
    hi = _bf(x)
    lo = _bf(x - hi.astype(F32))
    return _dot(m_bf, hi) + _dot(m_bf, lo)


def _sigmoid(x):
    return 1.0 / (1.0 + jnp.exp(-x))


def _silu(x):
    return x * _sigmoid(x)


def _layer_norm(x, w, b):
    mu = jnp.mean(x, axis=-1, keepdims=True)
    d = x - mu
    var = jnp.mean(d * d, axis=-1, keepdims=True)
    return d * lax.rsqrt(var + LN_EPS) * w + b


def _const_spec(shape):
    nd = len(shape)
    return pl.BlockSpec(shape, lambda *_: (0,) * nd, pipeline_mode=pl.Buffered(1))


def _row_spec(tt, d):
    return pl.BlockSpec((1, tt, d), lambda b, t: (b, t, 0))


def _seq_spec(rows, d):
    return pl.BlockSpec((1, rows, d), lambda b, t: (b, 0, 0))


def _ada_kernel(c_ref, w_ref, b_ref, o_ref):
    c = c_ref[...]
    o_ref[0] = _dot(_bf(_silu(c)), _bf(w_ref[0])) + b_ref[0]


def _ada_mod(c_all, ada_w, ada_b):
    s, d = c_all.shape
    depth, _, n = ada_w.shape
    tn = 1024
    return pl.pallas_call(
        _ada_kernel,
        grid=(depth, n // tn),
        in_specs=[pl.BlockSpec((s, d), lambda l, j: (0, 0)),
                  pl.BlockSpec((1, d, tn), lambda l, j: (l, 0, j)),
                  pl.BlockSpec((1, 1, tn), lambda l, j: (l, 0, j))],
        out_specs=pl.BlockSpec((1, s, tn), lambda l, j: (l, 0, j)),
        out_shape=jax.ShapeDtypeStruct((depth, s, n), F32),
        compiler_params=pltpu.CompilerParams(vmem_limit_bytes=_vmem(40)),
        name="ada_mod",
    )(c_all, ada_w, ada_b.reshape(depth, 1, n))


def _rwkv_proj_kernel(*refs, tt, has_vfirst):
    if has_vfirst:
        (x_ref, mod_ref, sp_ref, mu_ref, vec_ref, wr_ref, wk_ref, wv_ref, w1_ref, w2_ref, a1_ref, a2_ref,
         g1_ref, g2_ref, v1_ref, v2_ref, vf_ref,
         r_ref, lw_ref, k_ref, v_ref, kk_ref, a_ref, g_ref, ul_ref, carry_ref) = refs
    else:
        (x_ref, mod_ref, sp_ref, mu_ref, vec_ref, wr_ref, wk_ref, wv_ref, w1_ref, w2_ref, a1_ref, a2_ref,
         g1_ref, g2_ref,
         r_ref, lw_ref, k_ref, v_ref, kk_ref, a_ref, g_ref, ul_ref, carry_ref) = refs
    t = pl.program_id(1)

    @pl.when(t == 0)
    def _():
        carry_ref[0:1, :] = sp_ref[0]

    sh1 = mod_ref[0, 0:1, :]
    sc1 = mod_ref[0, 1:2, :]
    u = x_ref[0] * (1.0 + sc1) + sh1
    row = lax.broadcasted_iota(jnp.int32, u.shape, 0)
    prev = jnp.where(row == 0, carry_ref[0:1, :], pltpu.roll(u, 1, axis=0))
    carry_ref[0:1, :] = u[tt - 1:tt, :]
    ul_ref[0] = u[tt - 1:tt, :]
    xx = prev - u
    mix = lambda n: u + xx * mu_ref[n:n + 1, :]
    xr, xw, xk, xv, xa, xg = [mix(n) for n in range(6)]
    w0 = vec_ref[0:1, :]
    a0 = vec_ref[1:2, :]
    v0 = vec_ref[2:3, :]
    k_k = vec_ref[3:4, :]
    k_a = vec_ref[4:5, :]
    r = _dot(_bf(xr), wr_ref[...])
    k = _dot(_bf(xk), wk_ref[...])
    xvb = _bf(xv)
    v = _dot(xvb, wv_ref[...])
    z = w0 + _dot(_bf(jnp.tanh(_dot(_bf(xw), w1_ref[...]))), w2_ref[...])
    lw = (-math.exp(-0.5)) * _sigmoid(z)
    a = _sigmoid(a0 + _dot(_bf(_dot(_bf(xa), a1_ref[...])), a2_ref[...]))
    g = _dot(_bf(_sigmoid(_dot(_bf(xg), g1_ref[...]))), g2_ref[...])
    if has_vfirst:
        mv = _sigmoid(v0 + _dot(_bf(_dot(xvb, v1_ref[...])), v2_ref[...]))
        v = v + (vf_ref[0] - v) * mv
    r_ref[0] = r
    lw_ref[0] = lw
    k_ref[0] = k * (1.0 + (a - 1.0) * k_a)
    v_ref[0] = v
    kk_ref[0] = k * k_k
    a_ref[0] = a
    g_ref[0] = g


def _pad_cols(w, n):
    return jnp.pad(w, ((0, 0), (0, n - w.shape[1])))


def _pad_rows(w, n):
    return jnp.pad(w, ((0, n - w.shape[0]), (0, 0)))


def _rwkv_proj(x, mod, shift_prev, j, p, v_first, tt):
    b, t, d = x.shape
    has_vf = v_first is not None
    lora = lambda w1, w2: (_bf(_pad_cols(w1, -(-w1.shape[1] // LANES) * LANES)),
                           _bf(_pad_rows(w2, -(-w2.shape[0] // LANES) * LANES)))
    w1, w2 = lora(p['rwkv_w1'][j], p['rwkv_w2'][j])
    a1, a2 = lora(p['rwkv_a1'][j], p['rwkv_a2'][j])
    g1, g2 = lora(p['rwkv_g1'][j], p['rwkv_g2'][j])
    zero = jnp.zeros((d,), F32)
    vec = jnp.stack([p['rwkv_w0'][j], p['rwkv_a0'][j], p['rwkv_v0'][j - 1] if has_vf else zero,
                     p['rwkv_k_k'][j], p['rwkv_k_a'][j], zero, zero, zero])
    wrkv = _bf(p['rwkv_w_rkv'][j])
    ins = [x, mod, shift_prev.reshape(b, 1, d), p['rwkv_mu'][j], vec, wrkv[0], wrkv[1], wrkv[2],
           w1, w2, a1, a2, g1, g2]
    specs = [_row_spec(tt, d), _seq_spec(6, d), _seq_spec(1, d), _const_spec((6, d)), _const_spec((8, d)),
             _const_spec((d, d)), _const_spec((d, d)), _const_spec((d, d)),
             _const_spec(w1.shape), _const_spec(w2.shape), _const_spec(a1.shape), _const_spec(a2.shape),
             _const_spec(g1.shape), _const_spec(g2.shape)]
    if has_vf:
        v1, v2 = lora(p['rwkv_v1'][j - 1], p['rwkv_v2'][j - 1])
        ins += [v1, v2, v_first]
        specs += [_const_spec(v1.shape), _const_spec(v2.shape), _row_spec(tt, d)]
    big = jax.ShapeDtypeStruct((b, t, d), F32)
    outs = pl.pallas_call(
        functools.partial(_rwkv_proj_kernel, tt=tt, has_vfirst=has_vf),
        grid=(b, t // tt),
        in_specs=specs,
        out_specs=[_row_spec(tt, d)] * 7 + [_seq_spec(1, d)],
        out_shape=[big] * 7 + [jax.ShapeDtypeStruct((b, 1, d), F32)],
        scratch_shapes=[pltpu.VMEM((8, d), F32)],
        compiler_params=pltpu.CompilerParams(dimension_semantics=("arbitrary", "arbitrary"),
                                             vmem_limit_bytes=_vmem(56)),
        name="rwkv_proj",
    )(*ins)
    return outs


def _rwkv_rec_kernel(r_ref, lw_ref, k_ref, v_ref, kk_ref, a_ref, g_ref, s0_ref, tril_ref, bones_ref,
                     bdm_ref, ma_ref, mc_ref, eye_ref, vec_ref, o_ref, sout_ref, st_ref, *, L, nchunks):
    c = pl.program_id(2)
    hl = HEAD_GROUP * L
    nh = RWKV_HEAD

    @pl.when(c == 0)
    def _():
        st_ref[...] = jnp.zeros(st_ref.shape, F32)
        for h in range(HEAD_GROUP):
            st_ref[h * nh:(h + 1) * nh, h * nh:(h + 1) * nh] = s0_ref[0, h]

    r = r_ref[0]
    lw = lw_ref[0]
    k = k_ref[0]
    v = v_ref[0]
    kk = kk_ref[0]
    a = a_ref[0]
    bones = bones_ref[...]
    bdm = bdm_ref[...]
    cum = _dot01(tril_ref[...], lw)
    ss = _dot(_bf(kk * kk), bones)
    kap = kk * lax.rsqrt(jnp.maximum(ss, 1e-24))
    bv = kap * a
    cum_l = cum[L - 1:L, :]
    e_pos = jnp.exp(cum)
    e_neg = jnp.exp(-cum)
    e_rem = jnp.exp(cum_l - cum)
    bd = lambda x: jnp.concatenate([_bf(x)] * HEAD_GROUP, axis=0) * bdm
    lhs = jnp.concatenate([bd(kap * jnp.exp(cum - lw)), bd(r * e_pos)], axis=0)
    rhs = jnp.concatenate([bd(bv * e_neg), bd(k * e_neg)], axis=0)
    pm = _dot_nt(lhs, rhs)
    ma = ma_ref[...]
    mc = mc_ref[...]
    a_b = pm[:hl, :hl] * ma
    a_k = pm[:hl, hl:] * ma
    c_b = pm[hl:, :hl] * mc
    c_k = pm[hl:, hl:] * mc
    st = st_ref[...]
    ls = _dot_nt(lhs, _bf(st))
    vbd = bd(v)
    z = ls[:hl] + _dot(_bf(a_k), vbd)
    npow = -a_b
    tm = eye_ref[...] + npow
    for _ in range(int(math.log2(L)) - 1):
        nb = _bf(npow)
        npow = _dot(nb, nb)
        tm = tm + _dot(_bf(tm), _bf(npow))
    u = -_dot(_bf(tm), _bf(z))
    uv = jnp.concatenate([_bf(u), vbd], axis=0)
    obd = ls[hl:] + _dot(jnp.concatenate([_bf(c_b), _bf(c_k)], axis=1), uv)
    o = obd[0:L]
    for h in range(1, HEAD_GROUP):
        o = o + obd[h * L:(h + 1) * L]
    bk = jnp.concatenate([bd(bv * e_rem), bd(k * e_rem)], axis=0)
    st_new = st * jnp.exp(cum_l) + _dot_tn(uv, bk)
    st_ref[...] = st_new

    @pl.when(c == nchunks - 1)
    def _():
        for h in range(HEAD_GROUP):
            sout_ref[0, h] = st_new[h * nh:(h + 1) * nh, h * nh:(h + 1) * nh]

    inv = 1.0 / nh
    mean = _dot(_bf(o), bones) * inv
    d = o - mean
    var = _dot(_bf(d * d), bones) * inv
    gn_w = vec_ref[0:1, :]
    gn_b = vec_ref[1:2, :]
    r_k = vec_ref[2:3, :]
    on = d * lax.rsqrt(var + RWKV_GN_EPS) * gn_w + gn_b
    bonus = _dot(_bf(r * k * r_k), bones) * v
    o_ref[0] = (on + bonus) * g_ref[0]


def _rwkv_consts(L):
    hl = HEAD_GROUP * L
    idx = np.arange(hl)
    hh, tt = idx // L, idx % L
    same = hh[:, None] == hh[None, :]
    ma = (same & (tt[:, None] > tt[None, :])).astype(np.float32)
    mc = (same & (tt[:, None] >= tt[None, :])).astype(np.float32)
    eye = np.eye(hl, dtype=np.float32)
    lane_h = np.arange(GROUP_W) // RWKV_HEAD
    bdm = (hh[:, None] == lane_h[None, :]).astype(np.float32)
    bones = (lane_h[:, None] == lane_h[None, :]).astype(np.float32)
    tril = np.tril(np.ones((L, L), np.float32))
    return (jnp.asarray(tril, BF16), jnp.asarray(bones, BF16), jnp.asarray(bdm, BF16),
            jnp.asarray(ma), jnp.asarray(mc), jnp.asarray(eye))


def _rwkv_rec(r, lw, k, v, kk, a, g, s0, j, p, L):
    b, t, d = r.shape
    ng = d // GROUP_W
    nchunks = t // L
    tril, bones, bdm, ma, mc, eye = _rwkv_consts(L)
    zero = jnp.zeros((d,), F32)
    vec = jnp.stack([p['rwkv_gn_w'][j], p['rwkv_gn_b'][j], p['rwkv_r_k'][j].reshape(d)] + [zero] * 5)
    hl = HEAD_GROUP * L
    tile = pl.BlockSpec((1, L, GROUP_W), lambda bi, gi, ci: (bi, ci, gi))
    sspec = pl.BlockSpec((1, HEAD_GROUP, RWKV_HEAD, RWKV_HEAD), lambda bi, gi, ci: (bi, gi, 0, 0))
    cst = lambda shape: pl.BlockSpec(shape, lambda bi, gi, ci: (0,) * len(shape))
    o, s_out = pl.pallas_call(
        functools.partial(_rwkv_rec_kernel, L=L, nchunks=nchunks),
        grid=(b, ng, nchunks),
        in_specs=[tile] * 7 + [sspec, cst((L, L)), cst((GROUP_W, GROUP_W)), cst((hl, GROUP_W)),
                               cst((hl, hl)), cst((hl, hl)), cst((hl, hl)),
                               pl.BlockSpec((8, GROUP_W), lambda bi, gi, ci: (0, gi))],
        out_specs=[tile, sspec],
        out_shape=[jax.ShapeDtypeStruct((b, t, d), F32), jax.ShapeDtypeStruct(s0.shape, F32)],
        scratch_shapes=[pltpu.VMEM((GROUP_W, GROUP_W), F32)],
        compiler_params=pltpu.CompilerParams(dimension_semantics=("arbitrary", "arbitrary", "arbitrary"),
                                             vmem_limit_bytes=_vmem(48)),
        name="rwkv_rec",
    )(r, lw, k, v, kk, a, g, s0, tril, bones, bdm, ma, mc, eye, vec)
    return o, s_out


def _round_bf(x):
    return _bf(x).astype(F32)


def _rwkv_step_kernel(rr_ref, kr_ref, kkr_ref, ar_ref, lwr_ref, rc_ref, kc_ref, vc_ref, gc_ref, s0_ref,
                      vec_ref, o_ref, sout_ref, *, T, heads):
    for h in range(heads):
        r = rr_ref[0, h]
        k = kr_ref[0, h]
        kk = kkr_ref[0, h]
        w = jnp.exp(lwr_ref[0, h])
        kap = kk * lax.rsqrt(jnp.maximum(jnp.sum(kk * kk, axis=-1, keepdims=True), 1e-24))
        bv = kap * ar_ref[0, h]
        kapb = _round_bf(kap)
        rb = _round_bf(r)
        vc = vc_ref[0, h]
        lane = lax.broadcasted_iota(jnp.int32, vc.shape, 1)
        s = s0_ref[0, h]
        o = jnp.zeros(vc.shape, F32)
        for t in range(T):
            sa = jnp.sum(_round_bf(s) * kapb[t:t + 1], axis=-1, keepdims=True)
            s = s * w[t:t + 1] - sa * bv[t:t + 1] + vc[:, t:t + 1] * k[t:t + 1]
            o = jnp.where(lane == t, jnp.sum(_round_bf(s) * rb[t:t + 1], axis=-1, keepdims=True), o)
        sout_ref[0, h] = s
        gn_w = vec_ref[h, :, 0:1]
        gn_b = vec_ref[h, :, 1:2]
        r_k = vec_ref[h, :, 2:3]
        mean = jnp.mean(o, axis=0, keepdims=True)
        d = o - mean
        var = jnp.mean(d * d, axis=0, keepdims=True)
        on = d * lax.rsqrt(var + RWKV_GN_EPS) * gn_w + gn_b
        bonus = jnp.sum(rc_ref[0, h] * kc_ref[0, h] * r_k, axis=0, keepdims=True) * vc
        o_ref[0, h] = (on + bonus) * gc_ref[0, h]


def _rwkv_steps(r, lw, k, v, kk, a, g, s0, j, p):
    b, t, d = r.shape
    n = RWKV_HEAD
    h = d // n
    rows = lambda x: x.reshape(b, t, h, n).transpose(0, 2, 1, 3)
    cols = lambda x: x.reshape(b, t, h, n).transpose(0, 2, 3, 1)
    vec = jnp.stack([p['rwkv_gn_w'][j].reshape(h, n), p['rwkv_gn_b'][j].reshape(h, n), p['rwkv_r_k'][j]]
                    + [jnp.zeros((h, n), F32)] * 5, axis=-1)
    hb = STEP_HEADS
    rspec = pl.BlockSpec((1, hb, t, n), lambda bi, hi: (bi, hi, 0, 0))
    cspec = pl.BlockSpec((1, hb, n, t), lambda bi, hi: (bi, hi, 0, 0))
    sspec = pl.BlockSpec((1, hb, n, n), lambda bi, hi: (bi, hi, 0, 0))
    o, s_out = pl.pallas_call(
        functools.partial(_rwkv_step_kernel, T=t, heads=hb),
        grid=(b, h // hb),
        in_specs=[rspec] * 5 + [cspec] * 4 + [sspec, pl.BlockSpec((hb, n, 8), lambda bi, hi: (hi, 0, 0))],
        out_specs=[cspec, sspec],
        out_shape=[jax.ShapeDtypeStruct((b, h, n, t), F32), jax.ShapeDtypeStruct(s0.shape, F32)],
        compiler_params=pltpu.CompilerParams(dimension_semantics=("arbitrary", "arbitrary")),
        name="rwkv_steps",
    )(rows(r), rows(k), rows(kk), rows(a), rows(lw), cols(r), cols(k), cols(v), cols(g), s0, vec)
    return o.transpose(0, 3, 1, 2).reshape(b, t, d), s_out


def _hgrn_block_kernel(q_ref, k_ref, v_ref, g_ref, s0_ref, tril_ref, o_ref, sout_ref, *, T):
    q = q_ref[0]
    k = k_ref[0]
    v = v_ref[0]
    g = g_ref[0]
    tril = tril_ref[...]
    g_hi = _bf(g)
    g_r = g - g_hi.astype(F32)
    g_mid = _bf(g_r)
    g_lo = _bf(g_r - g_mid.astype(F32))
    b_inc = _dot(tril, g_hi) + _dot(tril, g_mid) + _dot(tril, g_lo)
    st = s0_ref[0, 0].T
    o = _dot_nt(_bf(q * jnp.exp(b_inc)), _bf(st))
    vb = _round_bf(v)
    row = lax.broadcasted_iota(jnp.int32, (T, 1), 0)
    for s in range(T):
        e = jnp.exp(jnp.where(row >= s, b_inc - b_inc[s:s + 1], -jnp.inf))
        col = jnp.sum((q * k[s:s + 1]) * e, axis=-1, keepdims=True)
        o = o + _round_bf(col) * vb[s:s + 1]
    b_last = b_inc[T - 1:T]
    st_new = st * jnp.exp(b_last) + _dot_tn(_bf(v), _bf(k * jnp.exp(b_last - b_inc)))
    sout_ref[0, 0] = st_new.T
    o_ref[0] = o * lax.rsqrt(jnp.mean(o * o, axis=-1, keepdims=True) + LN_EPS)


def _hgrn_block(q, k, v, g, s0):
    b, t, d = q.shape
    dk = HGRN_EXPAND
    nh = d // dk
    tril = jnp.asarray(np.tril(np.ones((t, t), np.float32)), BF16)
    tile = pl.BlockSpec((1, t, dk), lambda bi, hi: (bi, 0, hi))
    sspec = pl.BlockSpec((1, 1, dk, dk), lambda bi, hi: (bi, hi, 0, 0))
    return pl.pallas_call(
        functools.partial(_hgrn_block_kernel, T=t),
        grid=(b, nh),
        in_specs=[tile] * 4 + [sspec, pl.BlockSpec((t, t), lambda bi, hi: (0, 0))],
        out_specs=[tile, sspec],
        out_shape=[jax.ShapeDtypeStruct((b, t, d), F32), jax.ShapeDtypeStruct(s0.shape, F32)],
        compiler_params=pltpu.CompilerParams(dimension_semantics=("arbitrary", "arbitrary")),
        name="hgrn_block",
    )(q, k, v, g, s0, tril)


def _hgrn_levels(L):
    return [L >> (i + 1) for i in range(int(math.log2(L)))]


def _hgrn_rec_kernel(q_ref, k_ref, v_ref, g_ref, s0_ref, mall_ref, qsel_ref, ksel_ref, lmask_ref, eye_ref,
                     o_ref, sout_ref, st_ref, *, L, nchunks):
    c = pl.program_id(2)

    @pl.when(c == 0)
    def _():
        st_ref[...] = s0_ref[0, 0].T

    q = q_ref[0]
    k = k_ref[0]
    v = v_ref[0]
    g = g_ref[0]
    cs = _dot01(mall_ref[...], g)
    b_inc = cs[0:L]
    rev = cs[L:2 * L]
    st = st_ref[...]
    o = _dot_nt(_bf(q * jnp.exp(b_inc)), _bf(st))
    att = eye_ref[...] * jnp.sum(q * k, axis=-1, keepdims=True)
    for i in range(len(_hgrn_levels(L))):
        cq = cs[(2 + 2 * i) * L:(3 + 2 * i) * L]
        ck = cs[(3 + 2 * i) * L:(4 + 2 * i) * L]
        qh = q * jnp.exp(cq) * qsel_ref[i]
        kh = k * jnp.exp(ck) * ksel_ref[i]
        att = att + _dot_nt(_bf(qh), _bf(kh)) * lmask_ref[i]
    vb = _bf(v)
    o = o + _dot(_bf(att), vb)
    st_new = st * jnp.exp(b_inc[L - 1:L, :]) + _dot_tn(vb, _bf(k * jnp.exp(rev)))
    st_ref[...] = st_new

    @pl.when(c == nchunks - 1)
    def _():
        sout_ref[0, 0] = st_new.T

    o_ref[0] = o * lax.rsqrt(jnp.mean(o * o, axis=-1, keepdims=True) + LN_EPS)


def _hgrn_consts(L, dk):
    t = np.arange(L)
    rows = [np.tril(np.ones((L, L), np.float32)), np.triu(np.ones((L, L), np.float32), 1)]
    qsel, ksel, lmask = [], [], []
    for hs in _hgrn_levels(L):
        blk = t // (2 * hs)
        mid = blk * 2 * hs + hs - 1
        second = (t - blk * 2 * hs) >= hs
        i = t[None, :]
        rows.append((second[:, None] & (i > mid[:, None]) & (i <= t[:, None])).astype(np.float32))
        rows.append(((~second)[:, None] & (i > t[:, None]) & (i <= mid[:, None])).astype(np.float32))
        qsel.append(np.repeat(second.astype(np.float32)[:, None], dk, 1))
        ksel.append(np.repeat((~second).astype(np.float32)[:, None], dk, 1))
        lmask.append((blk[:, None] == blk[None, :]).astype(np.float32))
    return (jnp.asarray(np.concatenate(rows, 0), BF16), jnp.asarray(np.stack(qsel)),
            jnp.asarray(np.stack(ksel)), jnp.asarray(np.stack(lmask)), jnp.asarray(np.eye(L, dtype=np.float32)))


def _hgrn_rec(q, k, v, g, s0, L):
    b, t, d = q.shape
    dk = HGRN_EXPAND
    nh = d // dk
    nchunks = t // L
    mall, qsel, ksel, lmask, eye = _hgrn_consts(L, dk)
    nl = qsel.shape[0]
    tile = pl.BlockSpec((1, L, dk), lambda bi, hi, ci: (bi, ci, hi))
    sspec = pl.BlockSpec((1, 1, dk, dk), lambda bi, hi, ci: (bi, hi, 0, 0))
    cst = lambda shape: pl.BlockSpec(shape, lambda bi, hi, ci: (0,) * len(shape))
    o, s_out = pl.pallas_call(
        functools.partial(_hgrn_rec_kernel, L=L, nchunks=nchunks),
        grid=(b, nh, nchunks),
        in_specs=[tile] * 4 + [sspec, cst(mall.shape), cst((nl, L, dk)), cst((nl, L, dk)), cst((nl, L, L)),
                               cst((L, L))],
        out_specs=[tile, sspec],
        out_shape=[jax.ShapeDtypeStruct((b, t, d), F32), jax.ShapeDtypeStruct(s0.shape, F32)],
        scratch_shapes=[pltpu.VMEM((dk, dk), F32)],
        compiler_params=pltpu.CompilerParams(dimension_semantics=("arbitrary", "arbitrary", "arbitrary"),
                                             vmem_limit_bytes=_vmem(32)),
        name="hgrn_rec",
    )(q, k, v, g, s0, mall, qsel, ksel, lmask, eye)
    return o, s_out


def _hgrn_in_kernel(x_ref, mod_ref, lb_ref, w_ref, *out_refs, part):
    u = x_ref[0] * (1.0 + mod_ref[0, 1:2, :]) + mod_ref[0, 0:1, :]
    z = _dot(_bf(u), w_ref[...])
    if part == 1:
        lb = lb_ref[...]
        f = lb + (1.0 - lb) * _sigmoid(z)
        out_refs[0][0] = 1.0 - f
        out_refs[1][0] = jnp.log(f)
    elif part == 2:
        out_refs[0][0] = z
    else:
        out_refs[0][0] = _silu(z)


def _hgrn_in_part(x, mod, lb, w_part, part, tt):
    b, t, d = x.shape
    big = jax.ShapeDtypeStruct((b, t, d), F32)
    n_out = 2 if part == 1 else 1
    return pl.pallas_call(
        functools.partial(_hgrn_in_kernel, part=part),
        grid=(b, t // tt),
        in_specs=[_row_spec(tt, d), _seq_spec(6, d), _const_spec((1, d)), _const_spec((d, d))],
        out_specs=[_row_spec(tt, d)] * n_out,
        out_shape=[big] * n_out,
        compiler_params=pltpu.CompilerParams(dimension_semantics=("arbitrary", "arbitrary"),
                                             vmem_limit_bytes=_vmem(40)),
        name="hgrn_in",
    )(x, mod, lb.reshape(1, d), w_part)


def _hgrn_in(x, mod, lb, w_in, tt):
    d = x.shape[-1]
    w = _bf(w_in)
    (q,) = _hgrn_in_part(x, mod, lb, w[:, 0:d], 0, tt)
    kf, gl = _hgrn_in_part(x, mod, lb, w[:, d:2 * d], 1, tt)
    (vi,) = _hgrn_in_part(x, mod, lb, w[:, 2 * d:3 * d], 2, tt)
    (gate,) = _hgrn_in_part(x, mod, lb, w[:, 3 * d:4 * d], 3, tt)
    return q, kf, vi, gl, gate


def _pool_kernel(x_ref, mod_ref, hist_ref, w_ref, scale_ref, y_ref, hout_ref, ext_ref, *, tt, pos0):
    t = pl.program_id(1)
    hrows = POOL_HIST + 1
    d = x_ref.shape[-1]
    gw = d // len(POOL_WINDOWS)

    @pl.when(t == 0)
    def _():
        ext_ref[0:hrows, :] = hist_ref[0]

    @pl.when(t > 0)
    def _():
        ext_ref[0:hrows, :] = ext_ref[tt:tt + hrows, :]

    u = x_ref[0] * (1.0 + mod_ref[0, 1:2, :]) + mod_ref[0, 0:1, :]
    ext_ref[hrows:hrows + tt, :] = u
    hout_ref[0] = ext_ref[tt:tt + hrows, :]
    pos = pos0 + t * tt + lax.broadcasted_iota(jnp.int32, (tt, 1), 0)
    for gi, win in enumerate(POOL_WINDOWS):
        sl = slice(gi * gw, (gi + 1) * gw)
        tot = u[:, sl]
        for i in range(1, win):
            tot = tot + ext_ref[hrows - i:hrows - i + tt, sl]
        cnt = jnp.minimum(pos + 1, win).astype(F32)
        pooled = tot / cnt - u[:, sl]
        y_ref[0, :, sl] = _dot(_bf(pooled), w_ref[gi]) * scale_ref[:, sl]


def _pool_mix(x, mod, hist, j, p, pos0, tt):
    b, t, d = x.shape
    hrows = POOL_HIST + 1
    gw = d // len(POOL_WINDOWS)
    hist16 = jnp.pad(hist, ((0, 0), (1, 0), (0, 0)))
    y, hout = pl.pallas_call(
        functools.partial(_pool_kernel, tt=tt, pos0=pos0),
        grid=(b, t // tt),
        in_specs=[_row_spec(tt, d), _seq_spec(6, d), _seq_spec(hrows, d),
                  _const_spec((len(POOL_WINDOWS), gw, gw)), _const_spec((1, d))],
        out_specs=[_row_spec(tt, d), _seq_spec(hrows, d)],
        out_shape=[jax.ShapeDtypeStruct((b, t, d), F32), jax.ShapeDtypeStruct((b, hrows, d), F32)],
        scratch_shapes=[pltpu.VMEM((tt + hrows, d), F32)],
        compiler_params=pltpu.CompilerParams(dimension_semantics=("arbitrary", "arbitrary"),
                                             vmem_limit_bytes=_vmem(40)),
        name="pool_mix",
    )(x, mod, hist16, _bf(p['pool_w'][j]), p['pool_scale'][j].reshape(1, d))
    return y, hout[:, 1:, :]


def _tail_kernel(*refs, has_proj, has_scale):
    refs = list(refs)
    x_ref, act_ref = refs[0], refs[1]
    pos = 2
    if has_scale:
        scale_ref, gate_ref = refs[pos], refs[pos + 1]
        pos += 2
    if has_proj:
        wo_ref = refs[pos]
        pos += 1
    mod_ref, ln_ref, sg_ref, su_ref, sd_ref, rw_ref, x1_ref, h_ref, sh_ref, sc_ref = refs[pos:]
    act = act_ref[0]
    if has_scale:
        act = act * scale_ref[...] * gate_ref[0]
    y = _dot(_bf(act), wo_ref[...]) if has_proj else act
    g1 = mod_ref[0, 2:3, :]
    sh2 = mod_ref[0, 3:4, :]
    sc2 = mod_ref[0, 4:5, :]
    x1 = _layer_norm(ALPHA * x_ref[0] + (1.0 + g1) * y, ln_ref[0:1, :], ln_ref[1:2, :])
    x1_ref[0] = x1
    h = _bf(x1 * (1.0 + sc2) + sh2)
    h_ref[0] = h
    hid = _silu(_dot(h, sg_ref[...])) * _dot(h, su_ref[...])
    sh_ref[0] = _dot(_bf(hid), sd_ref[...])
    sc_ref[0] = _sigmoid(_dot(h, rw_ref[...]))


def _tail(x, act, mod, i, p, tt, w_out=None, scale=None, gate=None):
    b, t, d = x.shape
    has_proj = w_out is not None
    has_scale = scale is not None
    ln = jnp.concatenate([p['ln_w'][i], p['ln_b'][i]], axis=0)[jnp.array([0, 2, 1, 3])]
    ln = jnp.pad(ln, ((0, 4), (0, 0)))
    sg, su, sd = _bf(p['shared_w_gate'][i]), _bf(p['shared_w_up'][i]), _bf(p['shared_w_down'][i])
    rw = _bf(_pad_cols(p['router_w'][i], LANES))
    ins = [x, act]
    specs = [_row_spec(tt, d), _row_spec(tt, d)]
    if has_scale:
        ins += [scale.reshape(1, d), gate]
        specs += [_const_spec((1, d)), _row_spec(tt, d)]
    if has_proj:
        ins += [_bf(w_out)]
        specs += [_const_spec((d, d))]
    ins += [mod, ln, sg, su, sd, rw]
    specs += [_seq_spec(6, d), _const_spec((8, d)), _const_spec(sg.shape), _const_spec(su.shape),
              _const_spec(sd.shape), _const_spec(rw.shape)]
    return pl.pallas_call(
        functools.partial(_tail_kernel, has_proj=has_proj, has_scale=has_scale),
        grid=(b, t // tt),
        in_specs=specs,
        out_specs=[_row_spec(tt, d), _row_spec(tt, d), _row_spec(tt, d), _row_spec(tt, LANES)],
        out_shape=[jax.ShapeDtypeStruct((b, t, d), F32), jax.ShapeDtypeStruct((b, t, d), BF16),
                   jax.ShapeDtypeStruct((b, t, d), F32), jax.ShapeDtypeStruct((b, t, LANES), F32)],
        compiler_params=pltpu.CompilerParams(dimension_semantics=("arbitrary", "arbitrary"),
                                             vmem_limit_bytes=_vmem(56)),
        name="tail",
    )(*ins), ln


def _final_kernel(x1_ref, routed_ref, shared_ref, mod_ref, ln_ref, o_ref):
    g2 = mod_ref[0, 5:6, :]
    o_ref[0] = _layer_norm(ALPHA * x1_ref[0] + (1.0 + g2) * (routed_ref[...] + shared_ref[0]),
                           ln_ref[2:3, :], ln_ref[3:4, :])


def _final(x1, routed_flat, row0, shared, mod, ln, tt):
    b, t, d = x1.shape
    nt = t // tt
    base = row0 // tt
    return pl.pallas_call(
        _final_kernel,
        grid=(b, nt),
        in_specs=[_row_spec(tt, d), pl.BlockSpec((tt, d), lambda bi, ti: (base + bi * nt + ti, 0)),
                  _row_spec(tt, d), _seq_spec(6, d), _const_spec((8, d))],
        out_specs=_row_spec(tt, d),
        out_shape=jax.ShapeDtypeStruct((b, t, d), F32),
        compiler_params=pltpu.CompilerParams(dimension_semantics=("arbitrary", "arbitrary"),
                                             vmem_limit_bytes=_vmem(40)),
        name="final_ln",
    )(x1, routed_flat, shared, mod, ln)


def _moe_kernel(be_ref, nb_ref, x_ref, wg_ref, wu_ref, wd_ref, y_ref, wgb_ref, wub_ref, wdb_ref):
    i = pl.program_id(0)
    e = be_ref[i]
    prev = be_ref[jnp.maximum(i - 1, 0)]

    @pl.when((i == 0) | (e != prev))
    def _():
        wgb_ref[...] = _bf(wg_ref[0, 0])
        wub_ref[...] = _bf(wu_ref[0, 0])
        wdb_ref[...] = _bf(wd_ref[0, 0])

    @pl.when(i < nb_ref[0])
    def _():
        x = x_ref[...]
        hid = _silu(_dot(x, wgb_ref[...])) * _dot(x, wub_ref[...])
        y_ref[...] = _dot(_bf(hid), wdb_ref[...])

    @pl.when(i >= nb_ref[0])
    def _():
        y_ref[...] = jnp.zeros(y_ref.shape, y_ref.dtype)


def _moe_experts(x_sorted, blk_exp, n_used, w_gate, w_up, w_down, layer):
    n_slots, d = x_sorted.shape
    f = w_gate.shape[-1]
    bm = MOE_BM
    n_blocks = n_slots // bm
    grid_spec = pltpu.PrefetchScalarGridSpec(
        num_scalar_prefetch=2,
        grid=(n_blocks,),
        in_specs=[pl.BlockSpec((bm, d), lambda i, be, nb: (i, 0)),
                  pl.BlockSpec((1, 1, d, f), lambda i, be, nb: (layer, be[i], 0, 0)),
                  pl.BlockSpec((1, 1, d, f), lambda i, be, nb: (layer, be[i], 0, 0)),
                  pl.BlockSpec((1, 1, f, d), lambda i, be, nb: (layer, be[i], 0, 0))],
        out_specs=pl.BlockSpec((bm, d), lambda i, be, nb: (i, 0)),
        scratch_shapes=[pltpu.VMEM((d, f), BF16), pltpu.VMEM((d, f), BF16), pltpu.VMEM((f, d), BF16)],
    )
    return pl.pallas_call(
        _moe_kernel,
        grid_spec=grid_spec,
        out_shape=jax.ShapeDtypeStruct((n_slots, d), F32),
        compiler_params=pltpu.CompilerParams(dimension_semantics=("arbitrary",),
                                             vmem_limit_bytes=_vmem(58)),
        name="moe_experts",
    )(blk_exp, n_used, x_sorted, w_gate, w_up, w_down)


def _moe_routed(h_all, scores, bias, w_gate, w_up, w_down, layer):
    n_tok, d = h_all.shape
    n_exp = w_gate.shape[1]
    k = TOP_K
    bm = MOE_BM
    n_assign = n_tok * k
    n_blocks = -(-(n_assign + n_exp * (bm - 1)) // bm)
    _, top_idx = lax.top_k(scores + bias.astype(F32), k)
    top_s = jnp.take_along_axis(scores, top_idx, axis=-1)
    top_w = top_s / jnp.sum(top_s, axis=-1, keepdims=True) * ROUTED_SCALE
    flat_e = top_idx.reshape(-1).astype(jnp.int32)
    order = jnp.argsort(flat_e).astype(jnp.int32)
    e_sorted = flat_e[order]
    counts = jnp.bincount(flat_e, length=n_exp).astype(jnp.int32)
    padded = (counts + bm - 1) // bm * bm
    pad_end = jnp.cumsum(padded)
    pad_start = pad_end - padded
    start = jnp.cumsum(counts) - counts
    dest = pad_start[e_sorted] + jnp.arange(n_assign, dtype=jnp.int32) - start[e_sorted]
    slot_tok = jnp.full((n_blocks * bm,), n_tok, jnp.int32).at[dest].set(order // k)
    pos = jnp.zeros((n_assign,), jnp.int32).at[order].set(dest)
    blk_row0 = jnp.arange(n_blocks, dtype=jnp.int32) * bm
    blk_exp = jnp.minimum(jnp.sum((pad_end[None, :] <= blk_row0[:, None]).astype(jnp.int32), axis=1), n_exp - 1)
    n_used = (pad_end[-1:] // bm).astype(jnp.int32)
    h_pad = jnp.concatenate([h_all, jnp.zeros((1, d), h_all.dtype)], axis=0)
    x_sorted = h_pad[slot_tok]
    y_sorted = _moe_experts(x_sorted, blk_exp, n_used, w_gate, w_up, w_down, layer)
    y_tok = y_sorted[pos].reshape(n_tok, k, d)
    return jnp.sum(y_tok * top_w[:, :, None], axis=1)


def _row_tile(t):
    return min(t, 128)


def kernel(x_prompt, x_sample, state_rwkv_shift, state_rwkv_wkv, state_pool, state_hgrn, c_prompt, c_sample, ada_w, ada_b, ln_w, ln_b, rwkv_mu, rwkv_w_rkv, rwkv_w_o, rwkv_w0, rwkv_w1, rwkv_w2, rwkv_a0, rwkv_a1, rwkv_a2, rwkv_g1, rwkv_g2, rwkv_v0, rwkv_v1, rwkv_v2, rwkv_k_k, rwkv_k_a, rwkv_r_k, rwkv_gn_w, rwkv_gn_b, pool_w, pool_scale, hgrn_w_in, hgrn_norm_w, hgrn_w_out, hgrn_lb, router_w, router_bias, exp_w_gate, exp_w_up, exp_w_down, shared_w_gate, shared_w_up, shared_w_down):
    p = dict(ln_w=ln_w, ln_b=ln_b, rwkv_mu=rwkv_mu, rwkv_w_rkv=rwkv_w_rkv, rwkv_w_o=rwkv_w_o, rwkv_w0=rwkv_w0,
             rwkv_w1=rwkv_w1, rwkv_w2=rwkv_w2, rwkv_a0=rwkv_a0, rwkv_a1=rwkv_a1, rwkv_a2=rwkv_a2,
             rwkv_g1=rwkv_g1, rwkv_g2=rwkv_g2, rwkv_v0=rwkv_v0, rwkv_v1=rwkv_v1, rwkv_v2=rwkv_v2,
             rwkv_k_k=rwkv_k_k, rwkv_k_a=rwkv_k_a, rwkv_r_k=rwkv_r_k, rwkv_gn_w=rwkv_gn_w,
             rwkv_gn_b=rwkv_gn_b, pool_w=pool_w, pool_scale=pool_scale, hgrn_w_in=hgrn_w_in,
             hgrn_norm_w=hgrn_norm_w, hgrn_w_out=hgrn_w_out, router_w=router_w, router_bias=router_bias,
             shared_w_gate=shared_w_gate, shared_w_up=shared_w_up, shared_w_down=shared_w_down)
    depth = ada_w.shape[0]
    bp, tp, d = x_prompt.shape
    bs, ts, _ = x_sample.shape
    n_p, n_s = bp * tp, bs * ts

    lb_cum = jnp.cumsum(jax.nn.softmax(hgrn_lb.astype(F32), axis=0), axis=0)
    lower_bounds = lb_cum - lb_cum[0]

    n_seq = bp + bs
    c_all = jnp.pad(jnp.concatenate([c_prompt, c_sample], axis=0), ((0, -n_seq % 8), (0, 0)))
    mod_all = _ada_mod(c_all, ada_w, ada_b)

    zeros = lambda shape: jnp.zeros(shape, F32)
    groups = [
        dict(x=x_prompt, b=bp, t=tp, seq0=0, pos0=0, row0=0,
             shift=zeros((state_rwkv_shift.shape[0], bp, d)),
             wkv=zeros((state_rwkv_wkv.shape[0], bp) + state_rwkv_wkv.shape[2:]),
             pool=zeros((state_pool.shape[0], bp, POOL_HIST, d)),
             hgrn=zeros((state_hgrn.shape[0], bp) + state_hgrn.shape[2:])),
        dict(x=x_sample, b=bs, t=ts, seq0=bp, pos0=PAST_LEN, row0=n_p,
             shift=state_rwkv_shift, wkv=state_rwkv_wkv, pool=state_pool, hgrn=state_hgrn),
    ]
    for gr in groups:
        gr.update(shift_out=[], wkv_out=[], pool_out=[], hgrn_out=[], v_first=None)

    for i in range(depth):
        j = i // N_MIXERS
        kind = i % N_MIXERS
        tails = []
        for gr in groups:
            x, b, t = gr['x'], gr['b'], gr['t']
            tt = _row_tile(t)
            mod = mod_all[i, gr['seq0']:gr['seq0'] + b].reshape(b, 6, d)
            gr['mod'] = mod
            if kind == 0:
                r, lw, k, v, kk, a, g, ulast = _rwkv_proj(x, mod, gr['shift'][j], j, p, gr['v_first'], tt)
                if gr['v_first'] is None:
                    gr['v_first'] = v
                if t < CHUNK:
                    og, s_new = _rwkv_steps(r, lw, k, v, kk, a, g, gr['wkv'][j], j, p)
                else:
                    og, s_new = _rwkv_rec(r, lw, k, v, kk, a, g, gr['wkv'][j], j, p, CHUNK)
                gr['shift_out'].append(ulast[:, 0, :])
                gr['wkv_out'].append(s_new)
                outs, ln = _tail(x, og, mod, i, p, tt, w_out=rwkv_w_o[j])
            elif kind == 1:
                y, pl_new = _pool_mix(x, mod, gr['pool'][j], j, p, gr['pos0'], tt)
                gr['pool_out'].append(pl_new)
                outs, ln = _tail(x, y, mod, i, p, tt)
            else:
                q, kf, vi, gl, gate = _hgrn_in(x, mod, lower_bounds[i], hgrn_w_in[j], tt)
                if t < CHUNK:
                    o, h_new = _hgrn_block(q, kf, vi, gl, gr['hgrn'][j])
                else:
                    o, h_new = _hgrn_rec(q, kf, vi, gl, gr['hgrn'][j], CHUNK)
                gr['hgrn_out'].append(h_new)
                outs, ln = _tail(x, o, mod, i, p, tt, w_out=hgrn_w_out[j], scale=hgrn_norm_w[j], gate=gate)
            gr['x1'], gr['h'], gr['shared'], gr['scores'] = outs
            gr['ln'] = ln
        h_all = jnp.concatenate([gr['h'].reshape(-1, d) for gr in groups], axis=0)
        n_exp = router_w.shape[-1]
        scores = jnp.concatenate([gr['scores'].reshape(-1, LANES)[:, :n_exp] for gr in groups], axis=0)
        routed = _moe_routed(h_all, scores, router_bias[i], exp_w_gate, exp_w_up, exp_w_down, i)
        for gr in groups:
            tt = _row_tile(gr['t'])
            gr['x'] = _final(gr['x1'], routed, gr['row0'], gr['shared'], gr['mod'], gr['ln'], tt)

    outs = []
    for gr in groups:
        outs.append((gr['x'], jnp.stack(gr['shift_out']), jnp.stack(gr['wkv_out']),
                     jnp.stack(gr['pool_out']), jnp.stack(gr['hgrn_out'])))
    (yp, p_shift, p_wkv, p_pool, p_hgrn), (ys, s_shift, s_wkv, s_pool, s_hgrn) = outs
    return (yp, ys, p_shift, p_wkv, p_pool, p_hgrn, s_shift, s_wkv, s_pool, s_hgrn)
```

```python
import functools
import math

import numpy as np
import jax
import jax.numpy as jnp
from jax import lax
from jax.experimental import pallas as pl
from jax.experimental.pallas import tpu as pltpu

F32 = jnp.float32
BF16 = jnp.bfloat16

DEPTH = 4
N_MIXERS = 3
RWKV_HEAD = 64
RWKV_GN_EPS = 64e-5
POOL_WINDOWS = (2, 4, 8, 16)
POOL_HIST = max(POOL_WINDOWS) - 1
HGRN_EXPAND = 128
TOP_K = 8
ROUTED_SCALE = 2.5
ALPHA = (2 * DEPTH) ** 0.25
LN_EPS = 1e-5
CHUNK = 64
PAST_LEN = 2048

V7X_VMEM_BYTES = 64 * 1024 * 1024
LANES = 128
HEAD_GROUP = 4
GROUP_W = HEAD_GROUP * RWKV_HEAD
MOE_BM = 256
REC_GROUPS = 2
DMA_UNROLL = 8
ROW_TILE = 128
STEP_HEADS = 8


def _vmem(mb):
    return int(mb * 1024 * 1024)


def _bf(x):
    return x.astype(BF16)


def _dot(a, b):
    return jnp.dot(a, b, preferred_element_type=F32)


def _dot_nt(a, b):
    return lax.dot_general(a, b, (((1,), (1,)), ((), ())), preferred_element_type=F32)


def _dot_tn(a, b):
    return lax.dot_general(a, b, (((0,), (0,)), ((), ())), preferred_element_type=F32)


def _dot01(m_bf, x):
    hi = _bf(x)
    lo = _bf(x - hi.astype(F32))
    return _dot(m_bf, hi) + _dot(m_bf, lo)


def _sigmoid(x):
    return 1.0 / (1.0 + jnp.exp(-x))


def _silu(x):
    return x * (0.5 * jnp.tanh(0.5 * x) + 0.5)


def _layer_norm(x, w, b):
    mu = jnp.mean(x, axis=-1, keepdims=True)
    d = x - mu
    var = jnp.mean(d * d, axis=-1, keepdims=True)
    return d * lax.rsqrt(var + LN_EPS) * w + b


def _const_spec(shape):
    nd = len(shape)
    return pl.BlockSpec(shape, lambda *_: (0,) * nd, pipeline_mode=pl.Buffered(1))


def _row_spec(tt, d):
    return pl.BlockSpec((1, tt, d), lambda b, t: (b, t, 0))


def _seq_spec(rows, d):
    return pl.BlockSpec((1, rows, d), lambda b, t: (b, 0, 0))


def _ada_kernel(c_ref, w_ref, b_ref, o_ref):
    c = c_ref[...]
    o_ref[0] = _dot(_bf(_silu(c)), _bf(w_ref[0])) + b_ref[0]


def _ada_mod(c_all, ada_w, ada_b):
    s, d = c_all.shape
    depth, _, n = ada_w.shape
    tn = 1024
    return pl.pallas_call(
        _ada_kernel,
        grid=(depth, n // tn),
        in_specs=[pl.BlockSpec((s, d), lambda l, j: (0, 0)),
                  pl.BlockSpec((1, d, tn), lambda l, j: (l, 0, j)),
                  pl.BlockSpec((1, 1, tn), lambda l, j: (l, 0, j))],
        out_specs=pl.BlockSpec((1, s, tn), lambda l, j: (l, 0, j)),
        out_shape=jax.ShapeDtypeStruct((depth, s, n), F32),
        compiler_params=pltpu.CompilerParams(vmem_limit_bytes=_vmem(40)),
        name="ada_mod",
    )(c_all, ada_w, ada_b.reshape(depth, 1, n))


def _rwkv_proj_kernel(*refs, tt, has_vfirst):
    if has_vfirst:
        (x_ref, mod_ref, sp_ref, mu_ref, vec_ref, wr_ref, wk_ref, wv_ref, w1_ref, w2_ref, a1_ref, a2_ref,
         g1_ref, g2_ref, v1_ref, v2_ref, vf_ref,
         r_ref, lw_ref, k_ref, v_ref, kk_ref, a_ref, g_ref, ul_ref, carry_ref) = refs
    else:
        (x_ref, mod_ref, sp_ref, mu_ref, vec_ref, wr_ref, wk_ref, wv_ref, w1_ref, w2_ref, a1_ref, a2_ref,
         g1_ref, g2_ref,
         r_ref, lw_ref, k_ref, v_ref, kk_ref, a_ref, g_ref, ul_ref, carry_ref) = refs
    t = pl.program_id(1)

    @pl.when(t == 0)
    def _():
        carry_ref[0:1, :] = sp_ref[0]

    sh1 = mod_ref[0, 0:1, :]
    sc1 = mod_ref[0, 1:2, :]
    u = x_ref[0] * (1.0 + sc1) + sh1
    row = lax.broadcasted_iota(jnp.int32, u.shape, 0)
    prev = jnp.where(row == 0, carry_ref[0:1, :], pltpu.roll(u, 1, axis=0))
    carry_ref[0:1, :] = u[tt - 1:tt, :]
    ul_ref[0] = u[tt - 1:tt, :]
    xx = prev - u
    mix = lambda n: u + xx * mu_ref[n:n + 1, :]
    xr, xw, xk, xv, xa, xg = [mix(n) for n in range(6)]
    w0 = vec_ref[0:1, :]
    a0 = vec_ref[1:2, :]
    v0 = vec_ref[2:3, :]
    k_k = vec_ref[3:4, :]
    k_a = vec_ref[4:5, :]
    r = _dot(_bf(xr), wr_ref[...])
    k = _dot(_bf(xk), wk_ref[...])
    xvb = _bf(xv)
    v = _dot(xvb, wv_ref[...])
    z = w0 + _dot(_bf(jnp.tanh(_dot(_bf(xw), w1_ref[...]))), w2_ref[...])
    lw = (-math.exp(-0.5)) * _sigmoid(z)
    a = _sigmoid(a0 + _dot(_bf(_dot(_bf(xa), a1_ref[...])), a2_ref[...]))
    g = _dot(_bf(_sigmoid(_dot(_bf(xg), g1_ref[...]))), g2_ref[...])
    if has_vfirst:
        mv = _sigmoid(v0 + _dot(_bf(_dot(xvb, v1_ref[...])), v2_ref[...]))
        v = v + (vf_ref[0] - v) * mv
    r_ref[0] = r
    lw_ref[0] = lw
    k_ref[0] = k * (1.0 + (a - 1.0) * k_a)
    v_ref[0] = v
    kk_ref[0] = k * k_k
    a_ref[0] = a
    g_ref[0] = g


def _pad_cols(w, n):
    return jnp.pad(w, ((0, 0), (0, n - w.shape[1])))


def _pad_rows(w, n):
    return jnp.pad(w, ((0, n - w.shape[0]), (0, 0)))


def _rwkv_proj(x, mod, shift_prev, j, p, v_first, tt):
    b, t, d = x.shape
    has_vf = v_first is not None
    lora = lambda w1, w2: (_bf(_pad_cols(w1, -(-w1.shape[1] // LANES) * LANES)),
                           _bf(_pad_rows(w2, -(-w2.shape[0] // LANES) * LANES)))
    w1, w2 = lora(p['rwkv_w1'][j], p['rwkv_w2'][j])
    a1, a2 = lora(p['rwkv_a1'][j], p['rwkv_a2'][j])
    g1, g2 = lora(p['rwkv_g1'][j], p['rwkv_g2'][j])
    zero = jnp.zeros((d,), F32)
    vec = jnp.stack([p['rwkv_w0'][j], p['rwkv_a0'][j], p['rwkv_v0'][j - 1] if has_vf else zero,
                     p['rwkv_k_k'][j], p['rwkv_k_a'][j], zero, zero, zero])
    wrkv = _bf(p['rwkv_w_rkv'][j])
    ins = [x, mod, shift_prev.reshape(b, 1, d), p['rwkv_mu'][j], vec, wrkv[0], wrkv[1], wrkv[2],
           w1, w2, a1, a2, g1, g2]
    specs = [_row_spec(tt, d), _seq_spec(6, d), _seq_spec(1, d), _const_spec((6, d)), _const_spec((8, d)),
             _const_spec((d, d)), _const_spec((d, d)), _const_spec((d, d)),
             _const_spec(w1.shape), _const_spec(w2.shape), _const_spec(a1.shape), _const_spec(a2.shape),
             _const_spec(g1.shape), _const_spec(g2.shape)]
    if has_vf:
        v1, v2 = lora(p['rwkv_v1'][j - 1], p['rwkv_v2'][j - 1])
        ins += [v1, v2, v_first]
        specs += [_const_spec(v1.shape), _const_spec(v2.shape), _row_spec(tt, d)]
    big = jax.ShapeDtypeStruct((b, t, d), F32)
    outs = pl.pallas_call(
        functools.partial(_rwkv_proj_kernel, tt=tt, has_vfirst=has_vf),
        grid=(b, t // tt),
        in_specs=specs,
        out_specs=[_row_spec(tt, d)] * 7 + [_seq_spec(1, d)],
        out_shape=[big] * 7 + [jax.ShapeDtypeStruct((b, 1, d), F32)],
        scratch_shapes=[pltpu.VMEM((8, d), F32)],
        compiler_params=pltpu.CompilerParams(dimension_semantics=("arbitrary", "arbitrary"),
                                             vmem_limit_bytes=_vmem(56)),
        name="rwkv_proj",
    )(*ins)
    return outs


def _rwkv_chunk(r, lw, k, v, kk, a, g, st, tril, bones, bdm, ma, mc, eye, gn_w, gn_b, r_k, L):
    hl = HEAD_GROUP * L
    nh = RWKV_HEAD
    cum = _dot01(tril, lw)
    ss = _dot(_bf(kk * kk), bones)
    kap = kk * lax.rsqrt(jnp.maximum(ss, 1e-24))
    bv = kap * a
    cum_l = cum[L - 1:L, :]
    e_pos = jnp.exp(cum)
    e_neg = jnp.exp(-cum)
    e_rem = jnp.exp(cum_l - cum)
    bd = lambda x: jnp.concatenate([_bf(x)] * HEAD_GROUP, axis=0) * bdm
    lhs = jnp.concatenate([bd(kap * jnp.exp(cum - lw)), bd(r * e_pos)], axis=0)
    rhs = jnp.concatenate([bd(bv * e_neg), bd(k * e_neg)], axis=0)
    pm = _dot_nt(lhs, rhs)
    a_b = pm[:hl, :hl] * ma
    a_k = pm[:hl, hl:] * ma
    c_b = pm[hl:, :hl] * mc
    c_k = pm[hl:, hl:] * mc
    ls = _dot_nt(lhs, _bf(st))
    vbd = bd(v)
    z = ls[:hl] + _dot(_bf(a_k), vbd)
    npow = -a_b
    tm = eye + npow
    for _ in range(int(math.log2(L)) - 1):
        nb = _bf(npow)
        npow = _dot(nb, nb)
        tm = tm + _dot(_bf(tm), _bf(npow))
    u = -_dot(_bf(tm), _bf(z))
    uv = jnp.concatenate([_bf(u), vbd], axis=0)
    obd = ls[hl:] + _dot(jnp.concatenate([_bf(c_b), _bf(c_k)], axis=1), uv)
    o = obd[0:L]
    for h in range(1, HEAD_GROUP):
        o = o + obd[h * L:(h + 1) * L]
    bk = jnp.concatenate([bd(bv * e_rem), bd(k * e_rem)], axis=0)
    st_new = st * jnp.exp(cum_l) + _dot_tn(uv, bk)
    inv = 1.0 / nh
    mean = _dot(_bf(o), bones) * inv
    d = o - mean
    var = _dot(_bf(d * d), bones) * inv
    on = d * lax.rsqrt(var + RWKV_GN_EPS) * gn_w + gn_b
    bonus = _dot(_bf(r * k * r_k), bones) * v
    return (on + bonus) * g, st_new


def _rwkv_rec_kernel(r_ref, lw_ref, k_ref, v_ref, kk_ref, a_ref, g_ref, s0_ref, tril_ref, bones_ref,
                     bdm_ref, ma_ref, mc_ref, eye_ref, vec_ref, o_ref, sout_ref, st_ref, *, L, nchunks, groups):
    c = pl.program_id(2)
    nh = RWKV_HEAD

    @pl.when(c == 0)
    def _():
        st_ref[...] = jnp.zeros(st_ref.shape, F32)
        for gi in range(groups):
            for h in range(HEAD_GROUP):
                st_ref[gi, h * nh:(h + 1) * nh, h * nh:(h + 1) * nh] = s0_ref[0, gi * HEAD_GROUP + h]

    consts = (tril_ref[...], bones_ref[...], bdm_ref[...], ma_ref[...], mc_ref[...], eye_ref[...])
    for gi in range(groups):
        sl = slice(gi * GROUP_W, (gi + 1) * GROUP_W)
        tile = lambda ref: ref[0, :, sl]
        o, st_new = _rwkv_chunk(tile(r_ref), tile(lw_ref), tile(k_ref), tile(v_ref), tile(kk_ref), tile(a_ref),
                                tile(g_ref), st_ref[gi], *consts,
                                vec_ref[0:1, sl], vec_ref[1:2, sl], vec_ref[2:3, sl], L)
        o_ref[0, :, sl] = o
        st_ref[gi] = st_new

        @pl.when(c == nchunks - 1)
        def _():
            for h in range(HEAD_GROUP):
                sout_ref[0, gi * HEAD_GROUP + h] = st_new[h * nh:(h + 1) * nh, h * nh:(h + 1) * nh]


def _rwkv_consts(L):
    hl = HEAD_GROUP * L
    idx = np.arange(hl)
    hh, tt = idx // L, idx % L
    same = hh[:, None] == hh[None, :]
    ma = (same & (tt[:, None] > tt[None, :])).astype(np.float32)
    mc = (same & (tt[:, None] >= tt[None, :])).astype(np.float32)
    eye = np.eye(hl, dtype=np.float32)
    lane_h = np.arange(GROUP_W) // RWKV_HEAD
    bdm = (hh[:, None] == lane_h[None, :]).astype(np.float32)
    bones = (lane_h[:, None] == lane_h[None, :]).astype(np.float32)
    tril = np.tril(np.ones((L, L), np.float32))
    return (jnp.asarray(tril, BF16), jnp.asarray(bones, BF16), jnp.asarray(bdm, BF16),
            jnp.asarray(ma), jnp.asarray(mc), jnp.asarray(eye))


def _rwkv_rec(r, lw, k, v, kk, a, g, s0, j, p, L):
    b, t, d = r.shape
    groups = REC_GROUPS
    gw = groups * GROUP_W
    ng = d // gw
    nchunks = t // L
    tril, bones, bdm, ma, mc, eye = _rwkv_consts(L)
    zero = jnp.zeros((d,), F32)
    vec = jnp.stack([p['rwkv_gn_w'][j], p['rwkv_gn_b'][j], p['rwkv_r_k'][j].reshape(d)] + [zero] * 5)
    hl = HEAD_GROUP * L
    tile = pl.BlockSpec((1, L, gw), lambda bi, gi, ci: (bi, ci, gi))
    sspec = pl.BlockSpec((1, groups * HEAD_GROUP, RWKV_HEAD, RWKV_HEAD), lambda bi, gi, ci: (bi, gi, 0, 0))
    cst = lambda shape: pl.BlockSpec(shape, lambda bi, gi, ci: (0,) * len(shape))
    o, s_out = pl.pallas_call(
        functools.partial(_rwkv_rec_kernel, L=L, nchunks=nchunks, groups=groups),
        grid=(b, ng, nchunks),
        in_specs=[tile] * 7 + [sspec, cst((L, L)), cst((GROUP_W, GROUP_W)), cst((hl, GROUP_W)),
                               cst((hl, hl)), cst((hl, hl)), cst((hl, hl)),
                               pl.BlockSpec((8, gw), lambda bi, gi, ci: (0, gi))],
        out_specs=[tile, sspec],
        out_shape=[jax.ShapeDtypeStruct((b, t, d), F32), jax.ShapeDtypeStruct(s0.shape, F32)],
        scratch_shapes=[pltpu.VMEM((groups, GROUP_W, GROUP_W), F32)],
        compiler_params=pltpu.CompilerParams(dimension_semantics=("arbitrary", "arbitrary", "arbitrary"),
                                             vmem_limit_bytes=_vmem(48)),
        name="rwkv_rec",
    )(r, lw, k, v, kk, a, g, s0, tril, bones, bdm, ma, mc, eye, vec)
    return o, s_out


def _round_bf(x):
    return _bf(x).astype(F32)


def _rwkv_step_kernel(rr_ref, kr_ref, kkr_ref, ar_ref, lwr_ref, rc_ref, kc_ref, vc_ref, gc_ref, s0_ref,
                      vec_ref, o_ref, sout_ref, *, T):
    r = rr_ref[0]
    k = kr_ref[0]
    kk = kkr_ref[0]
    w = jnp.exp(lwr_ref[0])
    kap = kk * lax.rsqrt(jnp.maximum(jnp.sum(kk * kk, axis=-1, keepdims=True), 1e-24))
    bv = kap * ar_ref[0]
    kapb = _round_bf(kap)
    rb = _round_bf(r)
    vc = vc_ref[0]
    lane = lax.broadcasted_iota(jnp.int32, vc.shape, 2)
    s = s0_ref[0]
    o = jnp.zeros(vc.shape, F32)
    for t in range(T):
        sa = jnp.sum(_round_bf(s) * kapb[:, t:t + 1, :], axis=-1, keepdims=True)
        s = s * w[:, t:t + 1, :] - sa * bv[:, t:t + 1, :] + vc[:, :, t:t + 1] * k[:, t:t + 1, :]
        o = jnp.where(lane == t, jnp.sum(_round_bf(s) * rb[:, t:t + 1, :], axis=-1, keepdims=True), o)
    sout_ref[0] = s
    gn_w = vec_ref[:, :, 0:1]
    gn_b = vec_ref[:, :, 1:2]
    r_k = vec_ref[:, :, 2:3]
    mean = jnp.mean(o, axis=1, keepdims=True)
    d = o - mean
    var = jnp.mean(d * d, axis=1, keepdims=True)
    on = d * lax.rsqrt(var + RWKV_GN_EPS) * gn_w + gn_b
    bonus = jnp.sum(rc_ref[0] * kc_ref[0] * r_k, axis=1, keepdims=True) * vc
    o_ref[0] = (on + bonus) * gc_ref[0]


def _rwkv_steps(r, lw, k, v, kk, a, g, s0, j, p):
    b, t, d = r.shape
    n = RWKV_HEAD
    h = d // n
    rows = lambda x: x.reshape(b, t, h, n).transpose(0, 2, 1, 3)
    cols = lambda x: x.reshape(b, t, h, n).transpose(0, 2, 3, 1)
    vec = jnp.stack([p['rwkv_gn_w'][j].reshape(h, n), p['rwkv_gn_b'][j].reshape(h, n), p['rwkv_r_k'][j]]
                    + [jnp.zeros((h, n), F32)] * 5, axis=-1)
    hb = STEP_HEADS
    rspec = pl.BlockSpec((1, hb, t, n), lambda bi, hi: (bi, hi, 0, 0))
    cspec = pl.BlockSpec((1, hb, n, t), lambda bi, hi: (bi, hi, 0, 0))
    sspec = pl.BlockSpec((1, hb, n, n), lambda bi, hi: (bi, hi, 0, 0))
    o, s_out = pl.pallas_call(
        functools.partial(_rwkv_step_kernel, T=t),
        grid=(b, h // hb),
        in_specs=[rspec] * 5 + [cspec] * 4 + [sspec, pl.BlockSpec((hb, n, 8), lambda bi, hi: (hi, 0, 0))],
        out_specs=[cspec, sspec],
        out_shape=[jax.ShapeDtypeStruct((b, h, n, t), F32), jax.ShapeDtypeStruct(s0.shape, F32)],
        compiler_params=pltpu.CompilerParams(dimension_semantics=("arbitrary", "arbitrary")),
        name="rwkv_steps",
    )(rows(r), rows(k), rows(kk), rows(a), rows(lw), cols(r), cols(k), cols(v), cols(g), s0, vec)
    return o.transpose(0, 3, 1, 2).reshape(b, t, d), s_out


def _hgrn_block_kernel(q_ref, k_ref, v_ref, g_ref, s0_ref, tril_ref, o_ref, sout_ref, *, T):
    q = q_ref[0]
    k = k_ref[0]
    v = v_ref[0]
    g = g_ref[0]
    tril = tril_ref[...]
    g_hi = _bf(g)
    g_r = g - g_hi.astype(F32)
    g_mid = _bf(g_r)
    g_lo = _bf(g_r - g_mid.astype(F32))
    b_inc = _dot(tril, g_hi) + _dot(tril, g_mid) + _dot(tril, g_lo)
    st = s0_ref[0, 0].T
    o = _dot_nt(_bf(q * jnp.exp(b_inc)), _bf(st))
    vb = _round_bf(v)
    row = lax.broadcasted_iota(jnp.int32, (T, 1), 0)
    for s in range(T):
        e = jnp.exp(jnp.where(row >= s, b_inc - b_inc[s:s + 1], -jnp.inf))
        col = jnp.sum((q * k[s:s + 1]) * e, axis=-1, keepdims=True)
        o = o + _round_bf(col) * vb[s:s + 1]
    b_last = b_inc[T - 1:T]
    st_new = st * jnp.exp(b_last) + _dot_tn(_bf(v), _bf(k * jnp.exp(b_last - b_inc)))
    sout_ref[0, 0] = st_new.T
    o_ref[0] = o * lax.rsqrt(jnp.mean(o * o, axis=-1, keepdims=True) + LN_EPS)


def _hgrn_block(q, k, v, g, s0):
    b, t, d = q.shape
    dk = HGRN_EXPAND
    nh = d // dk
    tril = jnp.asarray(np.tril(np.ones((t, t), np.float32)), BF16)
    tile = pl.BlockSpec((1, t, dk), lambda bi, hi: (bi, 0, hi))
    sspec = pl.BlockSpec((1, 1, dk, dk), lambda bi, hi: (bi, hi, 0, 0))
    return pl.pallas_call(
        functools.partial(_hgrn_block_kernel, T=t),
        grid=(b, nh),
        in_specs=[tile] * 4 + [sspec, pl.BlockSpec((t, t), lambda bi, hi: (0, 0))],
        out_specs=[tile, sspec],
        out_shape=[jax.ShapeDtypeStruct((b, t, d), F32), jax.ShapeDtypeStruct(s0.shape, F32)],
        compiler_params=pltpu.CompilerParams(dimension_semantics=("arbitrary", "arbitrary")),
        name="hgrn_block",
    )(q, k, v, g, s0, tril)


def _hgrn_levels(L):
    return [L >> (i + 1) for i in range(int(math.log2(L)))]


def _hgrn_chunk(q, k, v, g, st, mall, qsel_ref, ksel_ref, lmask_ref, eye, L):
    cs = _dot01(mall, g)
    b_inc = cs[0:L]
    rev = cs[L:2 * L]
    o = _dot_nt(_bf(q * jnp.exp(b_inc)), _bf(st))
    att = eye * jnp.sum(q * k, axis=-1, keepdims=True)
    for i in range(len(_hgrn_levels(L))):
        cq = cs[(2 + 2 * i) * L:(3 + 2 * i) * L]
        ck = cs[(3 + 2 * i) * L:(4 + 2 * i) * L]
        qh = q * jnp.exp(cq) * qsel_ref[i]
        kh = k * jnp.exp(ck) * ksel_ref[i]
        att = att + _dot_nt(_bf(qh), _bf(kh)) * lmask_ref[i]
    vb = _bf(v)
    o = o + _dot(_bf(att), vb)
    st_new = st * jnp.exp(b_inc[L - 1:L, :]) + _dot_tn(vb, _bf(k * jnp.exp(rev)))
    return o * lax.rsqrt(jnp.mean(o * o, axis=-1, keepdims=True) + LN_EPS), st_new


def _hgrn_rec_kernel(q_ref, k_ref, v_ref, g_ref, s0_ref, mall_ref, qsel_ref, ksel_ref, lmask_ref, eye_ref,
                     o_ref, sout_ref, st_ref, *, L, nchunks, heads):
    c = pl.program_id(2)
    dk = HGRN_EXPAND

    @pl.when(c == 0)
    def _():
        for hi in range(heads):
            st_ref[hi] = s0_ref[0, hi].T

    mall = mall_ref[...]
    eye = eye_ref[...]
    for hi in range(heads):
        sl = slice(hi * dk, (hi + 1) * dk)
        o, st_new = _hgrn_chunk(q_ref[0, :, sl], k_ref[0, :, sl], v_ref[0, :, sl], g_ref[0, :, sl], st_ref[hi],
                                mall, qsel_ref, ksel_ref, lmask_ref, eye, L)
        o_ref[0, :, sl] = o
        st_ref[hi] = st_new

        @pl.when(c == nchunks - 1)
        def _():
            sout_ref[0, hi] = st_new.T


def _hgrn_consts(L, dk):
    t = np.arange(L)
    rows = [np.tril(np.ones((L, L), np.float32)), np.triu(np.ones((L, L), np.float32), 1)]
    qsel, ksel, lmask = [], [], []
    for hs in _hgrn_levels(L):
        blk = t // (2 * hs)
        mid = blk * 2 * hs + hs - 1
        second = (t - blk * 2 * hs) >= hs
        i = t[None, :]
        rows.append((second[:, None] & (i > mid[:, None]) & (i <= t[:, None])).astype(np.float32))
        rows.append(((~second)[:, None] & (i > t[:, None]) & (i <= mid[:, None])).astype(np.float32))
        qsel.append(np.repeat(second.astype(np.float32)[:, None], dk, 1))
        ksel.append(np.repeat((~second).astype(np.float32)[:, None], dk, 1))
        lmask.append((blk[:, None] == blk[None, :]).astype(np.float32))
    return (jnp.asarray(np.concatenate(rows, 0), BF16), jnp.asarray(np.stack(qsel)),
            jnp.asarray(np.stack(ksel)), jnp.asarray(np.stack(lmask)), jnp.asarray(np.eye(L, dtype=np.float32)))


def _hgrn_rec(q, k, v, g, s0, L):
    b, t, d = q.shape
    dk = HGRN_EXPAND
    nh = d // dk
    nchunks = t // L
    mall, qsel, ksel, lmask, eye = _hgrn_consts(L, dk)
    nl = qsel.shape[0]
    heads = 2 * REC_GROUPS
    tile = pl.BlockSpec((1, L, heads * dk), lambda bi, hi, ci: (bi, ci, hi))
    sspec = pl.BlockSpec((1, heads, dk, dk), lambda bi, hi, ci: (bi, hi, 0, 0))
    cst = lambda shape: pl.BlockSpec(shape, lambda bi, hi, ci: (0,) * len(shape))
    o, s_out = pl.pallas_call(
        functools.partial(_hgrn_rec_kernel, L=L, nchunks=nchunks, heads=heads),
        grid=(b, nh // heads, nchunks),
        in_specs=[tile] * 4 + [sspec, cst(mall.shape), cst((nl, L, dk)), cst((nl, L, dk)), cst((nl, L, L)),
                               cst((L, L))],
        out_specs=[tile, sspec],
        out_shape=[jax.ShapeDtypeStruct((b, t, d), F32), jax.ShapeDtypeStruct(s0.shape, F32)],
        scratch_shapes=[pltpu.VMEM((heads, dk, dk), F32)],
        compiler_params=pltpu.CompilerParams(dimension_semantics=("arbitrary", "arbitrary", "arbitrary"),
                                             vmem_limit_bytes=_vmem(32)),
        name="hgrn_rec",
    )(q, k, v, g, s0, mall, qsel, ksel, lmask, eye)
    return o, s_out


def _hgrn_in_kernel(x_ref, mod_ref, lb_ref, w_ref, *out_refs, part):
    u = x_ref[0] * (1.0 + mod_ref[0, 1:2, :]) + mod_ref[0, 0:1, :]
    z = _dot(_bf(u), w_ref[...])
    if part == 1:
        lb = lb_ref[...]
        f = lb + (1.0 - lb) * _sigmoid(z)
        out_refs[0][0] = 1.0 - f
        out_refs[1][0] = jnp.log(f)
    elif part == 2:
        out_refs[0][0] = z
    else:
        out_refs[0][0] = _silu(z)


def _hgrn_in_part(x, mod, lb, w_part, part, tt):
    b, t, d = x.shape
    big = jax.ShapeDtypeStruct((b, t, d), F32)
    n_out = 2 if part == 1 else 1
    return pl.pallas_call(
        functools.partial(_hgrn_in_kernel, part=part),
        grid=(b, t // tt),
        in_specs=[_row_spec(tt, d), _seq_spec(6, d), _const_spec((1, d)), _const_spec((d, d))],
        out_specs=[_row_spec(tt, d)] * n_out,
        out_shape=[big] * n_out,
        compiler_params=pltpu.CompilerParams(dimension_semantics=("arbitrary", "arbitrary"),
                                             vmem_limit_bytes=_vmem(40)),
        name="hgrn_in",
    )(x, mod, lb.reshape(1, d), w_part)


def _hgrn_in(x, mod, lb, w_in, tt):
    d = x.shape[-1]
    w = _bf(w_in)
    (q,) = _hgrn_in_part(x, mod, lb, w[:, 0:d], 0, tt)
    kf, gl = _hgrn_in_part(x, mod, lb, w[:, d:2 * d], 1, tt)
    (vi,) = _hgrn_in_part(x, mod, lb, w[:, 2 * d:3 * d], 2, tt)
    (gate,) = _hgrn_in_part(x, mod, lb, w[:, 3 * d:4 * d], 3, tt)
    return q, kf, vi, gl, gate


def _pool_kernel(x_ref, mod_ref, hist_ref, w_ref, scale_ref, y_ref, hout_ref, ext_ref, *, tt, pos0):
    t = pl.program_id(1)
    hrows = POOL_HIST + 1
    d = x_ref.shape[-1]
    gw = d // len(POOL_WINDOWS)

    @pl.when(t == 0)
    def _():
        ext_ref[0:hrows, :] = hist_ref[0]

    @pl.when(t > 0)
    def _():
        ext_ref[0:hrows, :] = ext_ref[tt:tt + hrows, :]

    u = x_ref[0] * (1.0 + mod_ref[0, 1:2, :]) + mod_ref[0, 0:1, :]
    ext_ref[hrows:hrows + tt, :] = u
    hout_ref[0] = ext_ref[tt:tt + hrows, :]
    pos = pos0 + t * tt + lax.broadcasted_iota(jnp.int32, (tt, 1), 0)
    for gi, win in enumerate(POOL_WINDOWS):
        sl = slice(gi * gw, (gi + 1) * gw)
        tot = u[:, sl]
        for i in range(1, win):
            tot = tot + ext_ref[hrows - i:hrows - i + tt, sl]
        cnt = jnp.minimum(pos + 1, win).astype(F32)
        pooled = tot / cnt - u[:, sl]
        y_ref[0, :, sl] = _dot(_bf(pooled), w_ref[gi]) * scale_ref[:, sl]


def _pool_mix(x, mod, hist, j, p, pos0, tt):
    b, t, d = x.shape
    hrows = POOL_HIST + 1
    gw = d // len(POOL_WINDOWS)
    hist16 = jnp.pad(hist, ((0, 0), (1, 0), (0, 0)))
    y, hout = pl.pallas_call(
        functools.partial(_pool_kernel, tt=tt, pos0=pos0),
        grid=(b, t // tt),
        in_specs=[_row_spec(tt, d), _seq_spec(6, d), _seq_spec(hrows, d),
                  _const_spec((len(POOL_WINDOWS), gw, gw)), _const_spec((1, d))],
        out_specs=[_row_spec(tt, d), _seq_spec(hrows, d)],
        out_shape=[jax.ShapeDtypeStruct((b, t, d), F32), jax.ShapeDtypeStruct((b, hrows, d), F32)],
        scratch_shapes=[pltpu.VMEM((tt + hrows, d), F32)],
        compiler_params=pltpu.CompilerParams(dimension_semantics=("arbitrary", "arbitrary"),
                                             vmem_limit_bytes=_vmem(40)),
        name="pool_mix",
    )(x, mod, hist16, _bf(p['pool_w'][j]), p['pool_scale'][j].reshape(1, d))
    return y, hout[:, 1:, :]


def _tail_kernel(*refs, has_proj, has_scale):
    refs = list(refs)
    x_ref, act_ref = refs[0], refs[1]
    pos = 2
    if has_scale:
        scale_ref, gate_ref = refs[pos], refs[pos + 1]
        pos += 2
    if has_proj:
        wo_ref = refs[pos]
        pos += 1
    mod_ref, ln_ref, sg_ref, su_ref, sd_ref, rw_ref, x1_ref, h_ref, sh_ref, sc_ref = refs[pos:]
    act = act_ref[0]
    if has_scale:
        act = act * scale_ref[...] * gate_ref[0]
    y = _dot(_bf(act), wo_ref[...]) if has_proj else act
    g1 = mod_ref[0, 2:3, :]
    sh2 = mod_ref[0, 3:4, :]
    sc2 = mod_ref[0, 4:5, :]
    x1 = _layer_norm(ALPHA * x_ref[0] + (1.0 + g1) * y, ln_ref[0:1, :], ln_ref[1:2, :])
    x1_ref[0] = x1
    hf = x1 * (1.0 + sc2) + sh2
    h_ref[0] = hf
    h = _bf(hf)
    hid = _silu(_dot(h, sg_ref[...])) * _dot(h, su_ref[...])
    sh_ref[0] = _dot(_bf(hid), sd_ref[...])
    sc_ref[0] = _sigmoid(_dot(h, rw_ref[...]))


def _tail(x, act, mod, i, p, tt, w_out=None, scale=None, gate=None):
    b, t, d = x.shape
    has_proj = w_out is not None
    has_scale = scale is not None
    ln = jnp.concatenate([p['ln_w'][i], p['ln_b'][i]], axis=0)[jnp.array([0, 2, 1, 3])]
    ln = jnp.pad(ln, ((0, 4), (0, 0)))
    sg, su, sd = _bf(p['shared_w_gate'][i]), _bf(p['shared_w_up'][i]), _bf(p['shared_w_down'][i])
    rw = _bf(_pad_cols(p['router_w'][i], LANES))
    ins = [x, act]
    specs = [_row_spec(tt, d), _row_spec(tt, d)]
    if has_scale:
        ins += [scale.reshape(1, d), gate]
        specs += [_const_spec((1, d)), _row_spec(tt, d)]
    if has_proj:
        ins += [_bf(w_out)]
        specs += [_const_spec((d, d))]
    ins += [mod, ln, sg, su, sd, rw]
    specs += [_seq_spec(6, d), _const_spec((8, d)), _const_spec(sg.shape), _const_spec(su.shape),
              _const_spec(sd.shape), _const_spec(rw.shape)]
    return pl.pallas_call(
        functools.partial(_tail_kernel, has_proj=has_proj, has_scale=has_scale),
        grid=(b, t // tt),
        in_specs=specs,
        out_specs=[_row_spec(tt, d), _row_spec(tt, d), _row_spec(tt, d), _row_spec(tt, LANES)],
        out_shape=[jax.ShapeDtypeStruct((b, t, d), F32), jax.ShapeDtypeStruct((b, t, d), F32),
                   jax.ShapeDtypeStruct((b, t, d), F32), jax.ShapeDtypeStruct((b, t, LANES), F32)],
        compiler_params=pltpu.CompilerParams(dimension_semantics=("arbitrary", "arbitrary"),
                                             vmem_limit_bytes=_vmem(56)),
        name="tail",
    )(*ins), ln


def _final_kernel(x1_ref, shared_ref, mod_ref, ln_ref, w_ref, *rest):
    y_refs, o_ref = rest[:-1], rest[-1]
    w = w_ref[...]
    routed = y_refs[0][...] * w[:, 0:1]
    for k in range(1, len(y_refs)):
        routed = routed + y_refs[k][...] * w[:, k:k + 1]
    g2 = mod_ref[0, 5:6, :]
    o_ref[0] = _layer_norm(ALPHA * x1_ref[0] + (1.0 + g2) * (routed + shared_ref[0]),
                           ln_ref[2:3, :], ln_ref[3:4, :])


def _final(x1, y_slots, top_w, n_stride, row0, shared, mod, ln, tt):
    b, t, d = x1.shape
    nt = t // tt
    k = top_w.shape[1]
    flat = lambda off: pl.BlockSpec((tt, d), lambda bi, ti: ((off + row0) // tt + bi * nt + ti, 0))
    return pl.pallas_call(
        _final_kernel,
        grid=(b, nt),
        in_specs=[_row_spec(tt, d), _row_spec(tt, d), _seq_spec(6, d), _const_spec((8, d)),
                  pl.BlockSpec((tt, k), lambda bi, ti: (row0 // tt + bi * nt + ti, 0))]
                 + [flat(kk * n_stride) for kk in range(k)],
        out_specs=_row_spec(tt, d),
        out_shape=jax.ShapeDtypeStruct((b, t, d), F32),
        compiler_params=pltpu.CompilerParams(dimension_semantics=("arbitrary", "arbitrary"),
                                             vmem_limit_bytes=_vmem(48)),
        name="final_ln",
    )(x1, shared, mod, ln, top_w, *([y_slots] * k))


def _moe_kernel(ue_ref, ublk_ref, ulo_ref, uhi_ref, tokc_ref, tokn_ref, dst_ref, h_hbm, wg_ref, wu_ref, wd_ref,
                y_hbm, xbuf, ybuf, wgb_ref, wub_ref, wdb_ref, gsem, ssem, *, bm, n_units, dump0):
    u = pl.program_id(0)
    up = jnp.maximum(u - 1, 0)
    e = ue_ref[u]
    blk = ublk_ref[u]
    lo = ulo_ref[u]
    hi = uhi_ref[u]
    nxt = ublk_ref[jnp.minimum(u + 1, n_units - 1)]
    slot = blk % 2
    ys = u % 2

    def gather_rows(tok_ref, s):
        def body(i, c):
            for j in range(DMA_UNROLL):
                r = i * DMA_UNROLL + j
                pltpu.make_async_copy(h_hbm.at[pl.ds(tok_ref[0, 0, r], 1)], xbuf.at[s, pl.ds(r, 1)],
                                      gsem.at[s]).start()
            return c
        lax.fori_loop(0, bm // DMA_UNROLL, body, 0)

    def wait_scatter(s):
        pltpu.make_async_copy(ybuf.at[s], y_hbm.at[pl.ds(0, bm)], ssem.at[s]).wait()

    @pl.when(u == 0)
    def _():
        gather_rows(tokc_ref, slot)

    @pl.when(nxt != blk)
    def _():
        gather_rows(tokn_ref, nxt % 2)

    @pl.when((u == 0) | (blk != ublk_ref[up]))
    def _():
        pltpu.make_async_copy(h_hbm.at[pl.ds(0, bm)], xbuf.at[slot], gsem.at[slot]).wait()

    @pl.when((u == 0) | (e != ue_ref[up]))
    def _():
        wgb_ref[...] = _bf(wg_ref[0, 0])
        wub_ref[...] = _bf(wu_ref[0, 0])
        wdb_ref[...] = _bf(wd_ref[0, 0])

    @pl.when(u >= 2)
    def _():
        wait_scatter(ys)

    x = _bf(xbuf[slot])
    hid = _silu(_dot(x, wgb_ref[...])) * _dot(x, wub_ref[...])
    ybuf[ys] = _dot(_bf(hid), wdb_ref[...])

    def scatter_rows(i, c):
        for j in range(DMA_UNROLL):
            r = i * DMA_UNROLL + j
            row = jnp.where((r >= lo) & (r < hi), dst_ref[0, 0, r], dump0 + ys * bm + r)
            pltpu.make_async_copy(ybuf.at[ys, pl.ds(r, 1)], y_hbm.at[pl.ds(row, 1)], ssem.at[ys]).start()
        return c
    lax.fori_loop(0, bm // DMA_UNROLL, scatter_rows, 0)

    @pl.when(u == n_units - 1)
    def _():
        wait_scatter(ys)
        if n_units >= 2:
            wait_scatter(1 - ys)


def _moe_experts(h_all, tok, dst, ue, ublk, ulo, uhi, w_gate, w_up, w_down, layer, n_out_rows):
    n_tok, d = h_all.shape
    f = w_gate.shape[-1]
    bm = MOE_BM
    n_units = ue.shape[0]
    last = n_units - 1
    idx = lambda which: pl.BlockSpec((1, 1, bm), which, memory_space=pltpu.SMEM)
    wspec = lambda shape: pl.BlockSpec((1, 1) + shape, lambda u, ue_, ub, ul, uh: (layer, ue_[u], 0, 0))
    grid_spec = pltpu.PrefetchScalarGridSpec(
        num_scalar_prefetch=4,
        grid=(n_units,),
        in_specs=[idx(lambda u, ue_, ub, ul, uh: (ub[u], 0, 0)),
                  idx(lambda u, ue_, ub, ul, uh: (ub[jnp.minimum(u + 1, last)], 0, 0)),
                  idx(lambda u, ue_, ub, ul, uh: (ub[u], 0, 0)),
                  pl.BlockSpec(memory_space=pl.ANY),
                  wspec((d, f)), wspec((d, f)), wspec((f, d))],
        out_specs=pl.BlockSpec(memory_space=pl.ANY),
        scratch_shapes=[pltpu.VMEM((2, bm, d), F32), pltpu.VMEM((2, bm, d), F32),
                        pltpu.VMEM((d, f), BF16), pltpu.VMEM((d, f), BF16), pltpu.VMEM((f, d), BF16),
                        pltpu.SemaphoreType.DMA((2,)), pltpu.SemaphoreType.DMA((2,))],
    )
    return pl.pallas_call(
        functools.partial(_moe_kernel, bm=bm, n_units=n_units, dump0=n_out_rows),
        grid_spec=grid_spec,
        out_shape=jax.ShapeDtypeStruct((n_out_rows + 2 * bm, d), F32),
        compiler_params=pltpu.CompilerParams(dimension_semantics=("arbitrary",), vmem_limit_bytes=_vmem(56),
                                             disable_bounds_checks=True),
        name="moe_experts",
    )(ue, ublk, ulo, uhi, tok, tok, dst, h_all, w_gate, w_up, w_down)


def _moe_routed(h_all, scores, bias, w_gate, w_up, w_down, layer):
    n_tok, d = h_all.shape
    n_exp = w_gate.shape[1]
    k = TOP_K
    bm = MOE_BM
    n_assign = n_tok * k
    n_blocks = -(-n_assign // bm)
    n_units = n_blocks + n_exp - 1
    i32 = jnp.int32
    _, top_idx = lax.top_k(scores + bias.astype(F32), k)
    top_s = jnp.take_along_axis(scores, top_idx, axis=-1)
    top_w = top_s / jnp.sum(top_s, axis=-1, keepdims=True) * ROUTED_SCALE
    flat_e = top_idx.reshape(-1).astype(i32)
    order = jnp.argsort(flat_e).astype(i32)
    order = jnp.pad(order, (0, n_blocks * bm - n_assign))
    tok = (order // k).reshape(n_blocks, 1, bm)
    n_stride = -(-n_tok // ROW_TILE) * ROW_TILE
    dst = ((order % k) * n_stride + order // k).reshape(n_blocks, 1, bm)
    eids = jnp.arange(n_exp, dtype=i32)
    counts = jnp.sum((flat_e[:, None] == eids[None, :]).astype(i32), axis=0)
    end = jnp.cumsum(counts)
    start = end - counts
    fb = start // bm
    nblk = jnp.where(counts > 0, (end - 1) // bm - fb + 1, 0)
    uend = jnp.cumsum(nblk)
    ustart = uend - nblk
    u = jnp.arange(n_units, dtype=i32)
    ue = jnp.minimum(jnp.sum((uend[None, :] <= u[:, None]).astype(i32), axis=1), n_exp - 1)
    onehot = (ue[:, None] == eids[None, :]).astype(i32)
    pick = lambda v: jnp.sum(onehot * v[None, :], axis=1)
    valid = u < uend[-1]
    ublk = pick(fb) + u - pick(ustart)
    ulo = jnp.clip(pick(start) - ublk * bm, 0, bm)
    uhi = jnp.clip(pick(end) - ublk * bm, 0, bm)
    e_last = jnp.max(jnp.where(counts > 0, eids, 0))
    ue = jnp.where(valid, ue, e_last)
    ublk = jnp.where(valid, ublk, n_blocks - 1)
    ulo = jnp.where(valid, ulo, 0)
    uhi = jnp.where(valid, uhi, 0)
    y_slots = _moe_experts(h_all, tok, dst, ue, ublk, ulo, uhi, w_gate, w_up, w_down, layer, k * n_stride)
    return y_slots, top_w, n_stride


def _row_tile(t):
    return min(t, ROW_TILE)


def kernel(x_prompt, x_sample, state_rwkv_shift, state_rwkv_wkv, state_pool, state_hgrn, c_prompt, c_sample, ada_w, ada_b, ln_w, ln_b, rwkv_mu, rwkv_w_rkv, rwkv_w_o, rwkv_w0, rwkv_w1, rwkv_w2, rwkv_a0, rwkv_a1, rwkv_a2, rwkv_g1, rwkv_g2, rwkv_v0, rwkv_v1, rwkv_v2, rwkv_k_k, rwkv_k_a, rwkv_r_k, rwkv_gn_w, rwkv_gn_b, pool_w, pool_scale, hgrn_w_in, hgrn_norm_w, hgrn_w_out, hgrn_lb, router_w, router_bias, exp_w_gate, exp_w_up, exp_w_down, shared_w_gate, shared_w_up, shared_w_down):
    p = dict(ln_w=ln_w, ln_b=ln_b, rwkv_mu=rwkv_mu, rwkv_w_rkv=rwkv_w_rkv, rwkv_w_o=rwkv_w_o, rwkv_w0=rwkv_w0,
             rwkv_w1=rwkv_w1, rwkv_w2=rwkv_w2, rwkv_a0=rwkv_a0, rwkv_a1=rwkv_a1, rwkv_a2=rwkv_a2,
             rwkv_g1=rwkv_g1, rwkv_g2=rwkv_g2, rwkv_v0=rwkv_v0, rwkv_v1=rwkv_v1, rwkv_v2=rwkv_v2,
             rwkv_k_k=rwkv_k_k, rwkv_k_a=rwkv_k_a, rwkv_r_k=rwkv_r_k, rwkv_gn_w=rwkv_gn_w,
             rwkv_gn_b=rwkv_gn_b, pool_w=pool_w, pool_scale=pool_scale, hgrn_w_in=hgrn_w_in,
             hgrn_norm_w=hgrn_norm_w, hgrn_w_out=hgrn_w_out, router_w=router_w, router_bias=router_bias,
             shared_w_gate=shared_w_gate, shared_w_up=shared_w_up, shared_w_down=shared_w_down)
    depth = ada_w.shape[0]
    bp, tp, d = x_prompt.shape
    bs, ts, _ = x_sample.shape
    n_p, n_s = bp * tp, bs * ts

    lb_cum = jnp.cumsum(jax.nn.softmax(hgrn_lb.astype(F32), axis=0), axis=0)
    lower_bounds = lb_cum - lb_cum[0]

    n_seq = bp + bs
    c_all = jnp.pad(jnp.concatenate([c_prompt, c_sample], axis=0), ((0, -n_seq % 8), (0, 0)))
    mod_all = _ada_mod(c_all, ada_w, ada_b)

    zeros = lambda shape: jnp.zeros(shape, F32)
    groups = [
        dict(x=x_prompt, b=bp, t=tp, seq0=0, pos0=0, row0=0,
             shift=zeros((state_rwkv_shift.shape[0], bp, d)),
             wkv=zeros((state_rwkv_wkv.shape[0], bp) + state_rwkv_wkv.shape[2:]),
             pool=zeros((state_pool.shape[0], bp, POOL_HIST, d)),
             hgrn=zeros((state_hgrn.shape[0], bp) + state_hgrn.shape[2:])),
        dict(x=x_sample, b=bs, t=ts, seq0=bp, pos0=PAST_LEN, row0=n_p,
             shift=state_rwkv_shift, wkv=state_rwkv_wkv, pool=state_pool, hgrn=state_hgrn),
    ]
    for gr in groups:
        gr.update(shift_out=[], wkv_out=[], pool_out=[], hgrn_out=[], v_first=None)

    for i in range(depth):
        j = i // N_MIXERS
        kind = i % N_MIXERS
        tails = []
        for gr in groups:
            x, b, t = gr['x'], gr['b'], gr['t']
            tt = _row_tile(t)
            mod = mod_all[i, gr['seq0']:gr['seq0'] + b].reshape(b, 6, d)
            gr['mod'] = mod
            if kind == 0:
                r, lw, k, v, kk, a, g, ulast = _rwkv_proj(x, mod, gr['shift'][j], j, p, gr['v_first'], tt)
                if gr['v_first'] is None:
                    gr['v_first'] = v
                if t < CHUNK:
                    og, s_new = _rwkv_steps(r, lw, k, v, kk, a, g, gr['wkv'][j], j, p)
                else:
                    og, s_new = _rwkv_rec(r, lw, k, v, kk, a, g, gr['wkv'][j], j, p, CHUNK)
                gr['shift_out'].append(ulast[:, 0, :])
                gr['wkv_out'].append(s_new)
                outs, ln = _tail(x, og, mod, i, p, tt, w_out=rwkv_w_o[j])
            elif kind == 1:
                y, pl_new = _pool_mix(x, mod, gr['pool'][j], j, p, gr['pos0'], tt)
                gr['pool_out'].append(pl_new)
                outs, ln = _tail(x, y, mod, i, p, tt)
            else:
                q, kf, vi, gl, gate = _hgrn_in(x, mod, lower_bounds[i], hgrn_w_in[j], tt)
                if t < CHUNK:
                    o, h_new = _hgrn_block(q, kf, vi, gl, gr['hgrn'][j])
                else:
                    o, h_new = _hgrn_rec(q, kf, vi, gl, gr['hgrn'][j], CHUNK)
                gr['hgrn_out'].append(h_new)
                outs, ln = _tail(x, o, mod, i, p, tt, w_out=hgrn_w_out[j], scale=hgrn_norm_w[j], gate=gate)
            gr['x1'], gr['h'], gr['shared'], gr['scores'] = outs
            gr['ln'] = ln
        h_all = jnp.concatenate([gr['h'].reshape(-1, d) for gr in groups], axis=0)
        n_exp = router_w.shape[-1]
        scores = jnp.concatenate([gr['scores'].reshape(-1, LANES)[:, :n_exp] for gr in groups], axis=0)
        y_slots, top_w, n_stride = _moe_routed(h_all, scores, router_bias[i], exp_w_gate, exp_w_up, exp_w_down, i)
        for gr in groups:
            tt = _row_tile(gr['t'])
            gr['x'] = _final(gr['x1'], y_slots, top_w, n_stride, gr['row0'], gr['shared'], gr['mod'], gr['ln'], tt)

    outs = []
    for gr in groups:
        outs.append((gr['x'], jnp.stack(gr['shift_out']), jnp.stack(gr['wkv_out']),
                     jnp.stack(gr['pool_out']), jnp.stack(gr['hgrn_out'])))
    (yp, p_shift, p_wkv, p_pool, p_hgrn), (ys, s_shift, s_wkv, s_pool, s_hgrn) = outs
    return (yp, ys, p_shift, p_wkv, p_pool, p_hgrn, s_shift, s_wkv, s_pool, s_hgrn)
```
